```python
import math
import jax
import jax.numpy as jnp
from jax import lax
import numpy as np

D_MODEL = 4096
BATCH = 1
SEQ = 8192
DEPTH = 4

GRID_W = 64
CTX_LEN = 256
N_MIXERS = 4
MOD_RANK = 256
NORM_EPS = 1e-6
CHUNK = 64
CONV_W = 4

RW_N = 64
RW_H = D_MODEL // RW_N
RW_DECAY_LORA = 128
RW_ICLR_LORA = 128
RW_GATE_LORA = 480
RW_GN_EPS = 64e-5

HG_K = 128
HG_H = D_MODEL // HG_K
HG_V = D_MODEL // HG_H

RG_W = D_MODEL
RG_H = 16
RG_BW = RG_W // RG_H
RG_C = 8.0

M2_DI = 2 * D_MODEL
M2_P = 64
M2_H = M2_DI // M2_P
M2_G = 8
M2_R = M2_H // M2_G
M2_N = 128
M2_CONV_DIM = M2_DI + 2 * M2_G * M2_N
M2_IN = 2 * M2_DI + 2 * M2_G * M2_N + 2 * M2_H

FFN_F = 2 * D_MODEL
N_EXPERTS = 8
TOP_K = 2
EXPERT_F = 3 * D_MODEL // 8

kernel_name = 'hybrid_bidir_rwkv7_hgrn2_rglru_mamba2_moe_dit'


def _rmsnorm(x, w):
    xf = x.astype(jnp.float32)
    y = xf * lax.rsqrt(jnp.mean(xf * xf, axis=-1, keepdims=True) + NORM_EPS)
    return (y * w.astype(jnp.float32)).astype(x.dtype)


def _modulation(cvec, down, up, bias):
    m = (jax.nn.silu(cvec) @ down) @ up + bias
    return m.reshape(cvec.shape[0], 6, D_MODEL)


def _modulate(n, shift, scale):
    return n * (1.0 + scale[:, None]) + shift[:, None]


def _heads(t, n):
    return t.reshape(t.shape[0], t.shape[1], -1, n)


def _conv_centred(x, w, b):
    y = lax.conv_general_dilated(
        x, w[:, None, :].astype(x.dtype), window_strides=(1,),
        padding=[(CONV_W // 2, CONV_W - 1 - CONV_W // 2)],
        dimension_numbers=('NWC', 'WIO', 'NWC'), feature_group_count=x.shape[-1])
    return y + b


def _token_shift_centred(x):
    xp = jnp.pad(x, ((0, 0), (1, 1), (0, 0)))
    return 0.5 * (xp[:, :-2] + xp[:, 2:]) - x


def _grid_to_scan(x, column_major):
    if not column_major:
        return x
    b_, t, d = x.shape
    rows = t // GRID_W
    return x.reshape(b_, rows, GRID_W, d).transpose(0, 2, 1, 3).reshape(b_, t, d)


def _scan_to_grid(x, column_major):
    if not column_major:
        return x
    b_, t, d = x.shape
    rows = t // GRID_W
    return x.reshape(b_, GRID_W, rows, d).transpose(0, 2, 1, 3).reshape(b_, t, d)


def _flip_time(tree):
    return jax.tree_util.tree_map(lambda a: jnp.flip(a, axis=1), tree)


def _ctx_then_latent(scan_fn, ctx_in, lat_in, s0, reverse):
    if reverse:
        ctx_in, lat_in = _flip_time(ctx_in), _flip_time(lat_in)
    y_c, s_c = scan_fn(ctx_in, s0)
    y_l, _ = scan_fn(lat_in, s_c)
    if reverse:
        y_c, y_l = _flip_time(y_c), _flip_time(y_l)
    return y_c, y_l


def _to_chunks(a):
    b_, t = a.shape[:2]
    return jnp.moveaxis(a.astype(jnp.float32).reshape((b_, t // CHUNK, CHUNK) + a.shape[2:]), 1, 0)


def _from_chunks(y):
    y = jnp.moveaxis(y, 0, 1)
    return y.reshape((y.shape[0], y.shape[1] * y.shape[2]) + y.shape[3:])


def _rwkv7_scan(inputs, s0):
    seq = tuple(jnp.moveaxis(t.astype(jnp.float32), 1, 0) for t in inputs)

    def step(s, z):
        r, w, k, v, a, b = z
        sa = jnp.einsum('bhvk,bhk->bhv', s, a)
        s = s * w[:, :, None, :] + sa[..., None] * b[:, :, None, :] + v[..., None] * k[:, :, None, :]
        return s, jnp.einsum('bhvk,bhk->bhv', s, r)

    s_fin, y = lax.scan(step, s0, seq)
    return jnp.moveaxis(y, 0, 1), s_fin


def _gla_chunked(inputs, s0):
    q, k, v, g = (_to_chunks(a) for a in inputs)
    mask = jnp.tril(jnp.ones((CHUNK, CHUNK), dtype=bool))[None, :, :, None, None]

    def step(s, z):
        qc, kc, vc, gc = z
        cum = jnp.cumsum(gc, axis=1)
        decay = jnp.exp(jnp.where(mask, cum[:, :, None] - cum[:, None, :], -jnp.inf))
        att = jnp.einsum('bthk,bshk,btshk->btsh', qc, kc, decay)
        o = jnp.einsum('btsh,bshv->bthv', att, vc) + jnp.einsum('bthk,bhkv->bthv', qc * jnp.exp(cum), s)
        last = cum[:, -1]
        s = s * jnp.exp(last)[..., None] + jnp.einsum(
            'bshk,bshv->bhkv', kc * jnp.exp(last[:, None] - cum), vc)
        return s, o

    s_fin, o = lax.scan(step, s0, (q, k, v, g))
    return _from_chunks(o), s_fin


def _diag_scan(inputs, h0):
    a, u = inputs

    def combine(left, right):
        return (left[0] * right[0], right[0] * left[1] + right[1])

    a_cum, h = lax.associative_scan(combine, (a, u), axis=1)
    h = h + a_cum * h0[:, None]
    return h, h[:, -1]


def _ssd_chunked(inputs, s0):
    x = inputs[0]
    mask = jnp.tril(jnp.ones((CHUNK, CHUNK), dtype=bool))[None, :, :, None, None]

    def step(s, z):
        xc, dtc, dac, bc, cc = z
        cum = jnp.cumsum(dac, axis=1)
        lmat = jnp.exp(jnp.where(mask, cum[:, :, None] - cum[:, None, :], -jnp.inf))
        m = jnp.einsum('btgn,bsgn->btsg', cc, bc)[..., None] * lmat * dtc[:, None]
        y = jnp.einsum('btsgr,bsgrp->btgrp', m, xc) + jnp.einsum(
            'btgn,bgrnp->btgrp', cc, s) * jnp.exp(cum)[..., None]
        last = cum[:, -1]
        s = s * jnp.exp(last)[..., None, None] + jnp.einsum(
            'bsgn,bsgr,bsgrp->bgrnp', bc, jnp.exp(last[:, None] - cum) * dtc, xc)
        return s, y

    s_fin, y = lax.scan(step, s0, tuple(_to_chunks(a) for a in inputs))
    return _from_chunks(y).reshape(x.shape), s_fin


def _rwkv7_mixer(u_c, u_l, need_ctx, mix, w_rkv, w0, w1, w2, a0, a1, a2, g1, g2,
                 k_k, k_a, r_k, lnx_w, lnx_b, w_o):
    k_a_h = k_a.reshape(RW_H, RW_N)

    def prep(u):
        xx = _token_shift_centred(u)
        r, k, v = jnp.einsum('nbtd,nde->nbte', u[None] + xx[None] * mix[:3, None, None], w_rkv)
        xw, xa, xg = (u + xx * mix[j] for j in (3, 4, 5))
        g = jax.nn.sigmoid(xg @ g1) @ g2
        kk = _heads((k * k_k).astype(jnp.float32), RW_N)
        kk = kk * lax.rsqrt(jnp.sum(kk * kk, axis=-1, keepdims=True) + 1e-12)
        rh, vh, kh = _heads(r, RW_N), _heads(v, RW_N), _heads(k.astype(jnp.float32), RW_N)
        per = []
        for d in range(2):
            w_log = -jax.nn.softplus(-(w0[d] + jnp.tanh(xw @ w1[d]) @ w2[d]).astype(jnp.float32)) - 0.5
            a = _heads(jax.nn.sigmoid((a0[d] + (xa @ a1[d]) @ a2[d]).astype(jnp.float32)), RW_N)
            kd = kh * (1.0 + (a - 1.0) * k_a_h)
            per.append((rh, _heads(jnp.exp(-jnp.exp(w_log)), RW_N), kd, vh, -kk, kk * a))
        return rh, vh, g, per

    pc, pl = prep(u_c), prep(u_l)
    s0 = jnp.zeros((u_l.shape[0], RW_H, RW_N, RW_N), jnp.float32)
    ys_c, ys_l = [], []
    for d in range(2):
        yc, yl = _ctx_then_latent(_rwkv7_scan, pc[3][d], pl[3][d], s0, d == 1)
        ys_c.append(yc)
        ys_l.append(yl)

    def out(p, ys):
        rh, vh, g, per = p
        y = ys[0] + ys[1]
        mu = jnp.mean(y, axis=-1, keepdims=True)
        var = jnp.mean(jnp.square(y - mu), axis=-1, keepdims=True)
        b_, t = y.shape[:2]
        yn = ((y - mu) * lax.rsqrt(var + RW_GN_EPS)).reshape(b_, t, D_MODEL) * lnx_w + lnx_b
        bonus_w = (jnp.sum(rh * per[0][2] * r_k, axis=-1, keepdims=True)
                   + jnp.sum(rh * per[1][2] * r_k, axis=-1, keepdims=True))
        o = (yn + (bonus_w * vh).reshape(b_, t, D_MODEL)) * g
        return o.astype(g.dtype) @ w_o

    y_c = out(pc, ys_c) if need_ctx else None
    return y_c, out(pl, ys_l)


def _hgrn2_mixer(u_c, u_l, need_ctx, layer, w_in, lb_logits, norm_w, w_o):
    cum = jnp.cumsum(jax.nn.softmax(lb_logits.astype(jnp.float32), axis=1), axis=1)
    lb = cum[:, layer] - cum[:, 0]

    def prep(u):
        q, i_in, og, zf_f, zf_b = jnp.split(u @ w_in, 5, axis=-1)
        qh, ih = _heads(jax.nn.silu(q), HG_K), _heads(i_in, HG_V)
        per = []
        for d, zf in enumerate((zf_f, zf_b)):
            zf = zf.astype(jnp.float32)
            k = (1.0 - lb[d]) * jax.nn.sigmoid(-zf)
            log_f = jnp.logaddexp(jnp.log(lb[d]), jnp.log1p(-lb[d]) + jax.nn.log_sigmoid(zf))
            per.append((qh, _heads(k, HG_K), ih, _heads(log_f, HG_K)))
        return og, per

    (og_c, pc), (og_l, pl) = prep(u_c), prep(u_l)
    s0 = jnp.zeros((u_l.shape[0], HG_H, HG_K, HG_V), jnp.float32)
    ys_c, ys_l = [], []
    for d in range(2):
        yc, yl = _ctx_then_latent(_gla_chunked, pc[d], pl[d], s0, d == 1)
        ys_c.append(yc)
        ys_l.append(yl)

    def out(og, ys):
        b_, t = og.shape[:2]
        o = _rmsnorm((ys[0] + ys[1]).reshape(b_, t, D_MODEL), norm_w) * jax.nn.silu(og.astype(jnp.float32))
        return o.astype(og.dtype) @ w_o

    y_c = out(og_c, ys_c) if need_ctx else None
    return y_c, out(og_l, ys_l)


def _rglru_mixer(u_c, u_l, need_ctx, w_in, conv_w, conv_b, wa, ba, wx, bx, lam, w_out):
    log_sig = jax.nn.log_sigmoid(lam.astype(jnp.float32))

    def prep(u):
        b_, t, _ = u.shape
        gate, xr = jnp.split(u @ w_in, 2, axis=-1)
        xc = _conv_centred(xr, conv_w, conv_b)
        xb = xc.reshape(b_, t, RG_H, RG_BW)
        per = []
        for d in range(2):
            r = jax.nn.sigmoid((jnp.einsum('bthi,hij->bthj', xb, wa[d]).reshape(b_, t, RG_W)
                                + ba[d]).astype(jnp.float32))
            ig = jax.nn.sigmoid((jnp.einsum('bthi,hij->bthj', xb, wx[d]).reshape(b_, t, RG_W)
                                 + bx[d]).astype(jnp.float32))
            log_a = RG_C * r * log_sig[d]
            per.append((jnp.exp(log_a), jnp.sqrt(-jnp.expm1(2.0 * log_a)) * ig * xc.astype(jnp.float32)))
        return gate, per

    (g_c, pc), (g_l, pl) = prep(u_c), prep(u_l)
    h0 = jnp.zeros((u_l.shape[0], RG_W), jnp.float32)
    hs_c, hs_l = [], []
    for d in range(2):
        hc, hl = _ctx_then_latent(_diag_scan, pc[d], pl[d], h0, d == 1)
        hs_c.append(hc)
        hs_l.append(hl)

    def out(gate, hs):
        return ((hs[0] + hs[1]).astype(gate.dtype) * jax.nn.gelu(gate)) @ w_out

    y_c = out(g_c, hs_c) if need_ctx else None
    return y_c, out(g_l, hs_l)


def _mamba2_mixer(u_c, u_l, need_ctx, w_in, conv_w, conv_b, dt_bias, a_log, d_skip, norm_w, w_out):
    a_neg = -jnp.exp(a_log.astype(jnp.float32)).reshape(2, M2_G, M2_R)

    def prep(u):
        b_, t, _ = u.shape
        z, xbc, dt_raw = jnp.split(u @ w_in, [M2_DI, M2_DI + M2_CONV_DIM], axis=-1)
        xbc = jax.nn.silu(_conv_centred(xbc, conv_w, conv_b))
        xs, bm, cm = jnp.split(xbc, [M2_DI, M2_DI + M2_G * M2_N], axis=-1)
        xs = xs.reshape(b_, t, M2_G, M2_R, M2_P)
        bm = bm.reshape(b_, t, M2_G, M2_N)
        cm = cm.reshape(b_, t, M2_G, M2_N)
        dt = jax.nn.softplus(dt_raw.astype(jnp.float32).reshape(b_, t, 2, M2_G, M2_R)
                             + dt_bias.reshape(2, M2_G, M2_R))
        per = [(xs, dt[:, :, d], dt[:, :, d] * a_neg[d], bm, cm) for d in range(2)]
        return z, xs, per

    pc, pl = prep(u_c), prep(u_l)
    s0 = jnp.zeros((u_l.shape[0], M2_G, M2_R, M2_N, M2_P), jnp.float32)
    ys_c, ys_l = [], []
    for d in range(2):
        yc, yl = _ctx_then_latent(_ssd_chunked, pc[2][d], pl[2][d], s0, d == 1)
        ys_c.append(yc)
        ys_l.append(yl)

    def out(p, ys):
        z, xs, _ = p
        b_, t = z.shape[:2]
        y = ys[0] + ys[1] + d_skip.reshape(M2_G, M2_R, 1) * xs
        y = y.reshape(b_, t, M2_DI) * jax.nn.silu(z.astype(jnp.float32))
        y = _rmsnorm(y.reshape(b_, t, M2_G, M2_DI // M2_G), norm_w.reshape(M2_G, M2_DI // M2_G))
        return y.reshape(b_, t, M2_DI).astype(z.dtype) @ w_out

    y_c = out(pc, ys_c) if need_ctx else None
    return y_c, out(pl, ys_l)


def _swiglu(u, w_gu, w_down):
    gt, up = jnp.split(u @ w_gu, 2, axis=-1)
    return (jax.nn.silu(gt) * up) @ w_down


def _moe_swiglu(u, router, w_gu, w_down):
    b_, t, d = u.shape
    xt = u.reshape(b_ * t, d)
    logits = (xt @ router).astype(jnp.float32)
    top_val, top_idx = lax.top_k(logits, TOP_K)
    gates = jax.nn.softmax(top_val, axis=-1)
    combine = jnp.sum(jax.nn.one_hot(top_idx, N_EXPERTS, dtype=jnp.float32) * gates[..., None], axis=1)
    gt, up = jnp.split(jnp.einsum('nd,edf->enf', xt, w_gu), 2, axis=-1)
    act = jax.nn.silu(gt) * up * combine.T[..., None].astype(u.dtype)
    return jnp.einsum('enf,efd->nd', act, w_down).reshape(b_, t, d)


def setup_inputs(seed: int = 0) -> dict:
    key = jax.random.key(seed)
    keys = iter(jax.random.split(key, 64))
    f32 = jnp.float32
    D = D_MODEL

    def nrm(shape, scale):
        return scale * jax.random.normal(next(keys), shape, f32)

    def unif(shape, lo, hi):
        return jax.random.uniform(next(keys), shape, f32, lo, hi)

    x = nrm((BATCH, SEQ, D), 1.0)
    c = nrm((BATCH, D), 1.0)
    ctx = nrm((BATCH, CTX_LEN, D), 1.0)
    c_ctx = nrm((D,), 1.0)
    mod_down = nrm((DEPTH, D, MOD_RANK), D ** -0.5)
    mod_up = nrm((DEPTH, MOD_RANK, 6 * D), 0.3 * MOD_RANK ** -0.5)
    mod_b = nrm((DEPTH, 6 * D), 0.02)
    norm_w = 1.0 + nrm((DEPTH, 2, D), 0.02)
    final_norm_w = 1.0 + nrm((D,), 0.02)
    rw_mix = unif((6, D), 0.1, 0.9)
    rw_w_rkv = nrm((3, D, D), D ** -0.5)
    rw_w0 = jnp.linspace(-6.0, -1.0, D, dtype=f32)[None] + nrm((2, D), 0.1)
    rw_w1 = nrm((2, D, RW_DECAY_LORA), D ** -0.5)
    rw_w2 = nrm((2, RW_DECAY_LORA, D), 0.5 * RW_DECAY_LORA ** -0.5)
    rw_a0 = nrm((2, D), 0.1)
    rw_a1 = nrm((2, D, RW_ICLR_LORA), D ** -0.5)
    rw_a2 = nrm((2, RW_ICLR_LORA, D), RW_ICLR_LORA ** -0.5)
    rw_g1 = nrm((D, RW_GATE_LORA), D ** -0.5)
    rw_g2 = nrm((RW_GATE_LORA, D), RW_GATE_LORA ** -0.5)
    rw_k_k = 0.85 + nrm((D,), 0.02)
    rw_k_a = 1.0 + nrm((D,), 0.02)
    rw_r_k = nrm((RW_H, RW_N), 0.1)
    rw_lnx_w = 1.0 + nrm((D,), 0.02)
    rw_lnx_b = nrm((D,), 0.02)
    rw_w_o = nrm((D, D), D ** -0.5)
    hg_w_in = nrm((D, 5 * D), D ** -0.5)
    hg_lb = nrm((2, DEPTH, D), 0.5)
    hg_norm_w = 1.0 + nrm((D,), 0.02)
    hg_w_o = nrm((D, D), D ** -0.5)
    rg_w_in = nrm((D, 2 * RG_W), D ** -0.5)
    rg_conv_w = nrm((CONV_W, RG_W), CONV_W ** -0.5)
    rg_conv_b = nrm((RG_W,), 0.02)
    rg_wa = nrm((2, RG_H, RG_BW, RG_BW), RG_BW ** -0.5)
    rg_ba = nrm((2, RG_W), 0.02)
    rg_wx = nrm((2, RG_H, RG_BW, RG_BW), RG_BW ** -0.5)
    rg_bx = nrm((2, RG_W), 0.02)
    rg_p = unif((2, RG_W), 0.9, 0.999) ** (1.0 / RG_C)
    rg_lam = jnp.log(rg_p) - jnp.log1p(-rg_p)
    rg_w_out = nrm((RG_W, D), RG_W ** -0.5)
    m2_w_in = nrm((D, M2_IN), D ** -0.5)
    m2_conv_w = nrm((CONV_W, M2_CONV_DIM), CONV_W ** -0.5)
    m2_conv_b = nrm((M2_CONV_DIM,), 0.02)
    dt0 = jnp.exp(unif((2, M2_H), math.log(1e-3), math.log(1e-1)))
    m2_dt_bias = dt0 + jnp.log(-jnp.expm1(-dt0))
    m2_a_log = jnp.log(unif((2, M2_H), 1.0, 16.0))
    m2_d = 1.0 + nrm((M2_H,), 0.1)
    m2_norm_w = 1.0 + nrm((M2_DI,), 0.02)
    m2_w_out = nrm((M2_DI, D), M2_DI ** -0.5)
    ffn_w_gu = nrm((2, D, 2 * FFN_F), D ** -0.5)
    ffn_w_down = nrm((2, FFN_F, D), FFN_F ** -0.5)
    moe_router = nrm((2, D, N_EXPERTS), D ** -0.5)
    moe_w_gu = nrm((2, N_EXPERTS, D, 2 * EXPERT_F), D ** -0.5)
    moe_w_down = nrm((2, N_EXPERTS, EXPERT_F, D), EXPERT_F ** -0.5)
    return {
        'x': x, 'c': c, 'ctx': ctx, 'c_ctx': c_ctx,
        'mod_down': mod_down, 'mod_up': mod_up, 'mod_b': mod_b,
        'norm_w': norm_w, 'final_norm_w': final_norm_w,
        'rw_mix': rw_mix, 'rw_w_rkv': rw_w_rkv, 'rw_w0': rw_w0, 'rw_w1': rw_w1, 'rw_w2': rw_w2,
        'rw_a0': rw_a0, 'rw_a1': rw_a1, 'rw_a2': rw_a2, 'rw_g1': rw_g1, 'rw_g2': rw_g2,
        'rw_k_k': rw_k_k, 'rw_k_a': rw_k_a, 'rw_r_k': rw_r_k, 'rw_lnx_w': rw_lnx_w,
        'rw_lnx_b': rw_lnx_b, 'rw_w_o': rw_w_o,
        'hg_w_in': hg_w_in, 'hg_lb': hg_lb, 'hg_norm_w': hg_norm_w, 'hg_w_o': hg_w_o,
        'rg_w_in': rg_w_in, 'rg_conv_w': rg_conv_w, 'rg_conv_b': rg_conv_b, 'rg_wa': rg_wa,
        'rg_ba': rg_ba, 'rg_wx': rg_wx, 'rg_bx': rg_bx, 'rg_lam': rg_lam, 'rg_w_out': rg_w_out,
        'm2_w_in': m2_w_in, 'm2_conv_w': m2_conv_w, 'm2_conv_b': m2_conv_b,
        'm2_dt_bias': m2_dt_bias, 'm2_a_log': m2_a_log, 'm2_d': m2_d, 'm2_norm_w': m2_norm_w,
        'm2_w_out': m2_w_out,
        'ffn_w_gu': ffn_w_gu, 'ffn_w_down': ffn_w_down,
        'moe_router': moe_router, 'moe_w_gu': moe_w_gu, 'moe_w_down': moe_w_down,
    }


def reference(x, c, ctx, c_ctx, mod_down, mod_up, mod_b, norm_w, final_norm_w,
              rw_mix, rw_w_rkv, rw_w0, rw_w1, rw_w2, rw_a0, rw_a1, rw_a2, rw_g1, rw_g2,
              rw_k_k, rw_k_a, rw_r_k, rw_lnx_w, rw_lnx_b, rw_w_o,
              hg_w_in, hg_lb, hg_norm_w, hg_w_o,
              rg_w_in, rg_conv_w, rg_conv_b, rg_wa, rg_ba, rg_wx, rg_bx, rg_lam, rg_w_out,
              m2_w_in, m2_conv_w, m2_conv_b, m2_dt_bias, m2_a_log, m2_d, m2_norm_w, m2_w_out,
              ffn_w_gu, ffn_w_down, moe_router, moe_w_gu, moe_w_down):
    h = ctx
    for i in range(DEPTH):
        kind = i % N_MIXERS
        column_major = i % 2 == 1
        need_ctx = i < DEPTH - 1
        mod_l = _modulation(c, mod_down[i], mod_up[i], mod_b[i])
        mod_c = _modulation(c_ctx[None], mod_down[i], mod_up[i], mod_b[i])

        u_l = _grid_to_scan(_modulate(_rmsnorm(x, norm_w[i, 0]), mod_l[:, 0], mod_l[:, 1]), column_major)
        u_c = _modulate(_rmsnorm(h, norm_w[i, 0]), mod_c[:, 0], mod_c[:, 1])
        if kind == 0:
            y_c, y_l = _rwkv7_mixer(u_c, u_l, need_ctx, rw_mix, rw_w_rkv, rw_w0, rw_w1, rw_w2,
                                    rw_a0, rw_a1, rw_a2, rw_g1, rw_g2, rw_k_k, rw_k_a, rw_r_k,
                                    rw_lnx_w, rw_lnx_b, rw_w_o)
        elif kind == 1:
            y_c, y_l = _hgrn2_mixer(u_c, u_l, need_ctx, i, hg_w_in, hg_lb, hg_norm_w, hg_w_o)
        elif kind == 2:
            y_c, y_l = _rglru_mixer(u_c, u_l, need_ctx, rg_w_in, rg_conv_w, rg_conv_b, rg_wa, rg_ba,
                                    rg_wx, rg_bx, rg_lam, rg_w_out)
        else:
            y_c, y_l = _mamba2_mixer(u_c, u_l, need_ctx, m2_w_in, m2_conv_w, m2_conv_b, m2_dt_bias,
                                     m2_a_log, m2_d, m2_norm_w, m2_w_out)
        x = x + mod_l[:, 2, None] * _scan_to_grid(y_l, column_major)
        if need_ctx:
            h = h + mod_c[:, 2, None] * y_c

        v = _modulate(_rmsnorm(x, norm_w[i, 1]), mod_l[:, 4], mod_l[:, 3])
        if need_ctx:
            v_c = _modulate(_rmsnorm(h, norm_w[i, 1]), mod_c[:, 4], mod_c[:, 3])
            v = jnp.concatenate([v_c, v], axis=1)
        fi = i // 2
        if i % 2 == 0:
            f = _swiglu(v, ffn_w_gu[fi], ffn_w_down[fi])
        else:
            f = _moe_swiglu(v, moe_router[fi], moe_w_gu[fi], moe_w_down[fi])
        x = x + mod_l[:, 5, None] * f[:, f.shape[1] - x.shape[1]:]
        if need_ctx:
            h = h + mod_c[:, 5, None] * f[:, :h.shape[1]]
    return _rmsnorm(x, final_norm_w)
```

```python
import functools
import math

import jax
import jax.numpy as jnp
from jax import lax
from jax.experimental import pallas as pl
from jax.experimental.pallas import tpu as pltpu

F32 = jnp.float32
BF16 = jnp.bfloat16

NORM_EPS = 1e-6
GRID_W = 64
CHUNK = 64
RW_N = 64
RW_GN_EPS = 64e-5
HG_K = 128
RG_H = 16
RG_C = 8.0
M2_P = 64
M2_G = 8
M2_N = 128
TOP_K = 2

_VMEM_LIMIT = 56 * 1024 * 1024

_NN = (((1,), (0,)), ((), ()))
_NT = (((1,), (1,)), ((), ()))
_TN = (((0,), (0,)), ((), ()))


def _params(n_axes):
    return pltpu.CompilerParams(dimension_semantics=("arbitrary",) * n_axes,
                                vmem_limit_bytes=_VMEM_LIMIT)


def _tile(n, prefs):
    for p in prefs:
        if n % p == 0:
            return p
    return n


def _bdot(a, b, dims=_NN):
    return lax.dot_general(a.astype(BF16), b.astype(BF16), dims, preferred_element_type=F32)


def _split3(x):
    hi = x.astype(BF16)
    r1 = x - hi.astype(F32)
    mid = r1.astype(BF16)
    lo = (r1 - mid.astype(F32)).astype(BF16)
    return hi, mid, lo


def _cumsum_rows(tri, x):
    hi, mid, lo = _split3(x)
    d = lambda p: jnp.dot(tri, p, preferred_element_type=F32)
    return d(hi) + d(mid) + d(lo)


def _cumsum_cols(x, triu):
    hi, mid, lo = _split3(x)
    d = lambda p: jnp.dot(p, triu, preferred_element_type=F32)
    return d(hi) + d(mid) + d(lo)


def _norm_mod_body(x_ref, w_ref, mod_ref, *o_refs, shift_i, scale_i, n_ctx, tm):
    x = x_ref[...]
    y = x * lax.rsqrt(jnp.mean(x * x, axis=-1, keepdims=True) + NORM_EPS) * w_ref[...]
    row = pl.program_id(0) * tm + lax.broadcasted_iota(jnp.int32, (tm, 1), 0)
    is_ctx = row < n_ctx
    shift = jnp.where(is_ctx, mod_ref[0, shift_i:shift_i + 1, :], mod_ref[1, shift_i:shift_i + 1, :])
    scale = jnp.where(is_ctx, mod_ref[0, scale_i:scale_i + 1, :], mod_ref[1, scale_i:scale_i + 1, :])
    y = y * (1.0 + scale) + shift
    for o in o_refs:
        o[...] = y.astype(o.dtype)


def _norm_mod(xh, w, mod, shift_i, scale_i, n_ctx, dtypes):
    m, d = xh.shape
    tm = _tile(m, (256, 128, 64, 8))
    out = pl.pallas_call(
        functools.partial(_norm_mod_body, shift_i=shift_i, scale_i=scale_i, n_ctx=n_ctx, tm=tm),
        grid=(m // tm,),
        in_specs=[pl.BlockSpec((tm, d), lambda i: (i, 0)),
                  pl.BlockSpec((1, d), lambda i: (0, 0)),
                  pl.BlockSpec((2, 6, d), lambda i: (0, 0, 0))],
        out_specs=[pl.BlockSpec((tm, d), lambda i: (i, 0)) for _ in dtypes],
        out_shape=[jax.ShapeDtypeStruct((m, d), dt) for dt in dtypes],
        compiler_params=_params(1),
    )(xh, w.reshape(1, d), mod)
    return out


def _rmsnorm_body(x_ref, w_ref, o_ref, *, groups):
    x = x_ref[...]
    gw = x.shape[-1] // groups
    for g in range(groups):
        xs = x[:, g * gw:(g + 1) * gw]
        o_ref[:, g * gw:(g + 1) * gw] = (
            xs * lax.rsqrt(jnp.mean(xs * xs, axis=-1, keepdims=True) + NORM_EPS)
            * w_ref[:, g * gw:(g + 1) * gw]).astype(o_ref.dtype)


def _rmsnorm(x, w, out_dtype, groups=1):
    m, d = x.shape
    tm = _tile(m, (256, 128, 64, 8))
    return pl.pallas_call(
        functools.partial(_rmsnorm_body, groups=groups),
        grid=(m // tm,),
        in_specs=[pl.BlockSpec((tm, d), lambda i: (i, 0)), pl.BlockSpec((1, d), lambda i: (0, 0))],
        out_specs=pl.BlockSpec((tm, d), lambda i: (i, 0)),
        out_shape=jax.ShapeDtypeStruct((m, d), out_dtype),
        compiler_params=_params(1),
    )(x, w.reshape(1, d))


def _mm_body(*refs, nw, nex, nk, epi, precision):
    x_ref = refs[0]
    w_refs = refs[1:1 + nw]
    ex_refs = refs[1 + nw:1 + nw + nex]
    o_ref = refs[1 + nw + nex]
    acc_refs = refs[2 + nw + nex:]
    x = x_ref[...]
    prods = [jnp.dot(x, w[...], preferred_element_type=F32, precision=precision) for w in w_refs]

    def finish(accs):
        o_ref[...] = epi(accs, [e[...] for e in ex_refs], pl.program_id(1)).astype(o_ref.dtype)

    if nk == 1:
        finish(prods)
    else:
        k = pl.program_id(2)

        @pl.when(k == 0)
        def _():
            for a, p in zip(acc_refs, prods):
                a[...] = p

        @pl.when(k > 0)
        def _():
            for a, p in zip(acc_refs, prods):
                a[...] += p

        @pl.when(k == nk - 1)
        def _():
            finish([a[...] for a in acc_refs])


def _matmul(x, ws, w_specs, *, n_out, epi, tm, tn, tk=None, extras=(), extra_specs=(),
            out_dtype=F32, x_map=None, k_total=None, precision=None):
    m = x.shape[0]
    k_total = k_total or x.shape[1]
    tk = tk or k_total
    nk = k_total // tk
    x_map = x_map or (lambda j, i, k: (i, k))
    return pl.pallas_call(
        functools.partial(_mm_body, nw=len(ws), nex=len(extras), nk=nk, epi=epi, precision=precision),
        grid=(n_out // tn, m // tm, nk),
        in_specs=[pl.BlockSpec((tm, tk), x_map)] + list(w_specs) + list(extra_specs),
        out_specs=pl.BlockSpec((tm, tn), lambda j, i, k: (i, j)),
        out_shape=jax.ShapeDtypeStruct((m, n_out), out_dtype),
        scratch_shapes=[pltpu.VMEM((tm, tn), F32) for _ in ws] if nk > 1 else [],
        compiler_params=_params(3),
    )(x, *ws, *extras)


def _epi_plain(accs, extras, i):
    return accs[0]


def _row_tile(m):
    return _tile(m, (768, 512, 640, 384, 256, 128, 64, 8))


def _k_tile(k):
    return k if k <= 4096 else _tile(k, (4096, 2048, 1024))


def _linear(x, w, out_dtype=F32, col0=0, n_out=None, precision=None):
    m, k = x.shape
    n_out = n_out or w.shape[1]
    tn = _tile(math.gcd(n_out, col0) if col0 else n_out, (512, 256, 128))
    tn = min(tn, n_out)
    c0 = col0 // tn
    tk = _k_tile(k)
    return _matmul(x, [w], [pl.BlockSpec((tk, tn), lambda j, i, kk: (kk, j + c0))],
                   n_out=n_out, epi=_epi_plain, tm=_row_tile(m), tn=tn, tk=tk,
                   out_dtype=out_dtype, precision=precision)


def _linear_residual(x, w, res, gate, n_ctx):
    m, k = x.shape
    n = w.shape[1]
    tm, tn, tk = _row_tile(m), _tile(n, (512, 256, 128)), _k_tile(k)

    def epi(accs, extras, i):
        r, g = extras
        row = i * tm + lax.broadcasted_iota(jnp.int32, (tm, 1), 0)
        return r + jnp.where(row < n_ctx, g[0:1, :], g[1:2, :]) * accs[0]

    return _matmul(x, [w], [pl.BlockSpec((tk, tn), lambda j, i, kk: (kk, j))],
                   n_out=n, epi=epi, tm=tm, tn=tn, tk=tk, extras=(res, gate),
                   extra_specs=(pl.BlockSpec((tm, tn), lambda j, i, kk: (i, j)),
                                pl.BlockSpec((2, tn), lambda j, i, kk: (0, j))))


def _epi_swiglu(accs, extras, i):
    g, u = accs
    return g * jax.nn.sigmoid(g) * u


def _swiglu_up(x, w_gu):
    m, k = x.shape
    f = w_gu.shape[1] // 2
    tn = _tile(f, (512, 256, 128))
    nj = f // tn
    return _matmul(x, [w_gu, w_gu],
                   [pl.BlockSpec((k, tn), lambda j, i, kk: (0, j)),
                    pl.BlockSpec((k, tn), lambda j, i, kk: (0, j + nj))],
                   n_out=f, epi=_epi_swiglu, tm=_row_tile(m), tn=tn, out_dtype=BF16)


def _moe_up(x, w_gu, comb_t):
    m, k = x.shape
    e, _, f2 = w_gu.shape
    f = f2 // 2
    tn = _tile(f, (512, 256, 128))
    nj = f // tn
    tm = _row_tile(m)

    def epi(accs, extras, i):
        g, u = accs
        return g * jax.nn.sigmoid(g) * u * extras[0]

    return _matmul(x, [w_gu, w_gu],
                   [pl.BlockSpec((None, k, tn), lambda j, i, kk: (j // nj, 0, j % nj)),
                    pl.BlockSpec((None, k, tn), lambda j, i, kk: (j // nj, 0, j % nj + nj))],
                   n_out=e * f, epi=epi, tm=tm, tn=tn, out_dtype=BF16,
                   extras=(comb_t,),
                   extra_specs=(pl.BlockSpec((None, tm, 1), lambda j, i, kk: (j // nj, i, 0)),))


def _block_diag(x, w):
    m = x.shape[0]
    h, bw, _ = w.shape
    return _matmul(x, [w], [pl.BlockSpec((None, bw, bw), lambda j, i, kk: (j, 0, 0))],
                   n_out=h * bw, epi=_epi_plain, tm=_row_tile(m), tn=bw, tk=bw,
                   x_map=lambda j, i, kk: (i, j), k_total=bw)


def _rwkv_body(r_ref, lw_ref, k_ref, v_ref, a_ref, b_ref, o_ref, s_ref, *, heads, n):
    c = CHUNK

    @pl.when(pl.program_id(2) == 0)
    def _():
        s_ref[...] = jnp.zeros_like(s_ref)

    ti = lax.broadcasted_iota(jnp.int32, (c, c), 0)
    si = lax.broadcasted_iota(jnp.int32, (c, c), 1)
    incl = ti >= si
    strict = ti > si
    tri = jnp.where(incl, 1.0, 0.0).astype(BF16)
    eye = jnp.where(ti == si, 1.0, 0.0)
    lw_all = lw_ref[...]
    cum_all = _cumsum_rows(tri, lw_all)
    for h in range(heads):
        sl = slice(h * n, (h + 1) * n)
        lw, cum = lw_all[:, sl], cum_all[:, sl]
        r, kd, v, a, b = r_ref[:, sl], k_ref[:, sl], v_ref[:, sl], a_ref[:, sl], b_ref[:, sl]
        s0 = s_ref[h]
        last = cum[c - 1:c, :]
        g_inv = jnp.exp(-cum)
        lhs = jnp.concatenate([a * jnp.exp(cum - lw), r * jnp.exp(cum)], axis=0)
        rhs = jnp.concatenate([b * g_inv, kd * g_inv], axis=0)
        p = _bdot(lhs, rhs, _NT)
        a_ab = jnp.where(strict, p[:c, :c], 0.0)
        a_ak = jnp.where(strict, p[:c, c:], 0.0)
        r_bk = jnp.concatenate([jnp.where(incl, p[c:, :c], 0.0), jnp.where(incl, p[c:, c:], 0.0)], axis=1)
        lm = _bdot(lhs, s0, _NT)
        t = eye + a_ab
        x = a_ab
        for _ in range(int(math.log2(c)) - 1):
            x = _bdot(x, x)
            t = t + _bdot(t, x)
        u = _bdot(t, lm[:c] + _bdot(a_ak, v))
        uv = jnp.concatenate([u, v], axis=0)
        o_ref[:, sl] = lm[c:] + _bdot(r_bk, uv)
        tail = jnp.exp(last - cum)
        bk = jnp.concatenate([b * tail, kd * tail], axis=0)
        s_ref[h] = s0 * jnp.exp(last) + _bdot(uv, bk, _TN)


def _rwkv_scan(r, lw, kd, v, a, b):
    _, t, d = r.shape
    heads = 8 if d % (8 * RW_N) == 0 else d // RW_N
    w = heads * RW_N
    spec = pl.BlockSpec((None, CHUNK, w), lambda dr, g, c: (dr, c, g))
    return pl.pallas_call(
        functools.partial(_rwkv_body, heads=heads, n=RW_N),
        grid=(2, d // w, t // CHUNK),
        in_specs=[spec] * 6,
        out_specs=spec,
        out_shape=jax.ShapeDtypeStruct((2, t, d), F32),
        scratch_shapes=[pltpu.VMEM((heads, RW_N, RW_N), F32)],
        compiler_params=_params(3),
    )(r, lw, kd, v, a, b)


def _gla_body(q_ref, k_ref, v_ref, g_ref, o_ref, s_ref, *, heads, kdim, sub):
    c = CHUNK

    @pl.when(pl.program_id(2) == 0)
    def _():
        s_ref[...] = jnp.zeros_like(s_ref)

    ti = lax.broadcasted_iota(jnp.int32, (c, c), 0)
    si = lax.broadcasted_iota(jnp.int32, (c, c), 1)
    tri = jnp.where(ti >= si, 1.0, 0.0).astype(BF16)
    cum_all = _cumsum_rows(tri, g_ref[...])
    for h in range(heads):
        sl = slice(h * kdim, (h + 1) * kdim)
        q, k, v, cum = q_ref[:, sl], k_ref[:, sl], v_ref[:, sl], cum_all[:, sl]
        s0 = s_ref[h]
        last = cum[c - 1:c, :]
        inter = _bdot(q * jnp.exp(cum), s0, _NT)
        for i in range(c // sub):
            lo, hi = i * sub, (i + 1) * sub
            piv = cum[lo - 1:lo, :] if i > 0 else jnp.zeros_like(last)
            qi = q[lo:hi] * jnp.exp(cum[lo:hi] - piv)
            ki = k[:hi] * jnp.exp(piv - cum[:hi])
            att = _bdot(qi, ki, _NT)
            tt = lo + lax.broadcasted_iota(jnp.int32, (sub, hi), 0)
            ss = lax.broadcasted_iota(jnp.int32, (sub, hi), 1)
            att = jnp.where(tt >= ss, att, 0.0)
            o_ref[lo:hi, sl] = inter[lo:hi] + _bdot(att, v[:hi])
        s_ref[h] = s0 * jnp.exp(last) + _bdot(v, k * jnp.exp(last - cum), _TN)


def _gla_scan(q, k, v, g):
    _, t, d = q.shape
    heads = 4 if d % (4 * HG_K) == 0 else d // HG_K
    w = heads * HG_K
    spec = pl.BlockSpec((None, CHUNK, w), lambda dr, gi, c: (dr, c, gi))
    return pl.pallas_call(
        functools.partial(_gla_body, heads=heads, kdim=HG_K, sub=16),
        grid=(2, d // w, t // CHUNK),
        in_specs=[spec] * 4,
        out_specs=spec,
        out_shape=jax.ShapeDtypeStruct((2, t, d), F32),
        scratch_shapes=[pltpu.VMEM((heads, HG_K, HG_K), F32)],
        compiler_params=_params(3),
    )(q, k, v, g)


def _lru_body(a_ref, u_ref, o_ref, h_ref, *, tb):
    @pl.when(pl.program_id(2) == 0)
    def _():
        h_ref[...] = jnp.zeros_like(h_ref)

    def step(t, h):
        h = a_ref[pl.ds(t, 1), :] * h + u_ref[pl.ds(t, 1), :]
        o_ref[pl.ds(t, 1), :] = h
        return h

    h_ref[0:1, :] = lax.fori_loop(0, tb, step, h_ref[0:1, :], unroll=8)


def _lru_scan(a, u):
    _, t, w = a.shape
    tb = _tile(t, (256, 128, 64, 8))
    lb = _tile(w, (512, 256, 128))
    spec = pl.BlockSpec((None, tb, lb), lambda dr, j, i: (dr, i, j))
    return pl.pallas_call(
        functools.partial(_lru_body, tb=tb),
        grid=(2, w // lb, t // tb),
        in_specs=[spec, spec],
        out_specs=spec,
        out_shape=jax.ShapeDtypeStruct((2, t, w), F32),
        scratch_shapes=[pltpu.VMEM((8, lb), F32)],
        compiler_params=_params(3),
    )(a, u)


def _ssd_body(x_ref, b_ref, c_ref, dtc_ref, dac_ref, dtr_ref, dar_ref, o_ref, s_ref, xw_ref, dec_ref, *, r, p):
    c = CHUNK

    @pl.when(pl.program_id(2) == 0)
    def _():
        s_ref[...] = jnp.zeros_like(s_ref)

    ti = lax.broadcasted_iota(jnp.int32, (c, c), 0)
    si = lax.broadcasted_iota(jnp.int32, (c, c), 1)
    incl = ti >= si
    tri = jnp.where(incl, 1.0, 0.0).astype(BF16)
    triu = jnp.where(si >= ti, 1.0, 0.0).astype(BF16)
    cum_col = _cumsum_rows(tri, dac_ref[...])
    cum_row = _cumsum_cols(dar_ref[...], triu)
    dt_col = dtc_ref[...]
    dt_row = dtr_ref[...]
    bm, cm = b_ref[...], c_ref[...]
    cb = _bdot(cm, bm, _NT)
    cs = _bdot(cm, s_ref[...])
    for h in range(r):
        sl = slice(h * p, (h + 1) * p)
        cc = cum_col[:, h:h + 1]
        cr = cum_row[h:h + 1, :]
        last = cum_col[c - 1:c, h:h + 1]
        lmat = jnp.exp(jnp.where(incl, cc - cr, -jnp.inf))
        m = cb * lmat * dt_row[h:h + 1, :]
        xh = x_ref[:, sl]
        o_ref[:, sl] = _bdot(m, xh) + cs[:, sl] * jnp.exp(cc)
        xw_ref[:, sl] = (xh * (jnp.exp(last - cc) * dt_col[:, h:h + 1])).astype(BF16)
        dec_ref[:, sl] = jnp.broadcast_to(jnp.exp(last), (8, p))
    s_ref[...] = s_ref[...] * dec_ref[0:1, :] + _bdot(bm, xw_ref[...], _TN)


def _ssd_scan(x, bm, cm, dt, da):
    _, t, di = x.shape
    g = M2_G
    r = di // (g * M2_P)
    gw = r * M2_P
    nch = t // CHUNK
    col = lambda z: z.reshape(2, t, g, r).transpose(0, 2, 1, 3)
    row = lambda z: z.reshape(2, nch, CHUNK, g, r).transpose(0, 3, 1, 4, 2)
    xspec = pl.BlockSpec((None, CHUNK, gw), lambda dr, gi, c: (dr, c, gi))
    nspec = pl.BlockSpec((None, CHUNK, M2_N), lambda dr, gi, c: (dr, c, gi))
    cspec = pl.BlockSpec((None, None, CHUNK, r), lambda dr, gi, c: (dr, gi, c, 0))
    rspec = pl.BlockSpec((None, None, None, r, CHUNK), lambda dr, gi, c: (dr, gi, c, 0, 0))
    return pl.pallas_call(
        functools.partial(_ssd_body, r=r, p=M2_P),
        grid=(2, g, nch),
        in_specs=[xspec, nspec, nspec, cspec, cspec, rspec, rspec],
        out_specs=xspec,
        out_shape=jax.ShapeDtypeStruct((2, t, di), F32),
        scratch_shapes=[pltpu.VMEM((M2_N, gw), F32), pltpu.VMEM((CHUNK, gw), BF16), pltpu.VMEM((8, gw), F32)],
        compiler_params=_params(3),
    )(x, bm, cm, col(dt), col(da), row(dt), row(da))


def _rev(x, n_ctx):
    return jnp.concatenate([jnp.flip(x[:n_ctx], axis=0), jnp.flip(x[n_ctx:], axis=0)], axis=0)


def _both(x, n_ctx):
    return jnp.stack([x, _rev(x, n_ctx)])


def _merge(y, n_ctx):
    return y[0] + _rev(y[1], n_ctx)


def _to_scan(x, n_ctx, column_major):
    if not column_major:
        return x
    lat = x[n_ctx:]
    t, d = lat.shape
    lat = lat.reshape(t // GRID_W, GRID_W, d).transpose(1, 0, 2).reshape(t, d)
    return jnp.concatenate([x[:n_ctx], lat], axis=0)


def _to_grid(x, n_ctx, column_major):
    if not column_major:
        return x
    lat = x[n_ctx:]
    t, d = lat.shape
    lat = lat.reshape(GRID_W, t // GRID_W, d).transpose(1, 0, 2).reshape(t, d)
    return jnp.concatenate([x[:n_ctx], lat], axis=0)


def _shifted(x, n_ctx, off):
    ts = x.shape[0]
    t = jnp.arange(ts)[:, None]
    lo = jnp.where(t < n_ctx, 0, n_ctx)
    hi = jnp.where(t < n_ctx, n_ctx, ts)
    ok = (t + off >= lo) & (t + off < hi)
    return jnp.where(ok, jnp.roll(x, -off, axis=0), 0.0)


def _conv_centred(x, w, b, n_ctx):
    return sum(w[j] * _shifted(x, n_ctx, j - 2) for j in range(w.shape[0])) + b


def _rwkv7_mixer(u, n_ctx, mix, w_rkv, w0, w1, w2, a0, a1, a2, g1, g2, k_k, k_a, r_k, lnx_w, lnx_b):
    ts, d = u.shape
    hn = d // RW_N
    xx = 0.5 * (_shifted(u, n_ctx, -1) + _shifted(u, n_ctx, 1)) - u
    xs = [(u + xx * mix[j]).astype(BF16) for j in range(6)]
    r, k, v = (_linear(xs[j], w_rkv[j].astype(BF16)) for j in range(3))
    gate = _linear(jax.nn.sigmoid(_linear(xs[5], g1.astype(BF16))).astype(BF16), g2.astype(BF16))
    heads = lambda z: z.reshape(ts, hn, RW_N)
    kk = heads(k * k_k)
    kk = (kk * lax.rsqrt(jnp.sum(kk * kk, axis=-1, keepdims=True) + 1e-12)).reshape(ts, d)
    lws, kds, bs = [], [], []
    for dr in range(2):
        z = w0[dr] + _linear(jnp.tanh(_linear(xs[3], w1[dr].astype(BF16))).astype(BF16), w2[dr].astype(BF16))
        lws.append(-math.exp(-0.5) * jax.nn.sigmoid(z))
        a = jax.nn.sigmoid(a0[dr] + _linear(_linear(xs[4], a1[dr].astype(BF16)).astype(BF16), a2[dr].astype(BF16)))
        kds.append(k * (1.0 + (a - 1.0) * k_a))
        bs.append(kk * a)
    seq = lambda f, b_: jnp.stack([f, _rev(b_, n_ctx)])
    y = _rwkv_scan(seq(r, r), seq(lws[0], lws[1]), seq(kds[0], kds[1]), seq(v, v),
                   seq(-kk, -kk), seq(bs[0], bs[1]))
    y = heads(_merge(y, n_ctx))
    mu = jnp.mean(y, axis=-1, keepdims=True)
    var = jnp.mean(jnp.square(y - mu), axis=-1, keepdims=True)
    yn = ((y - mu) * lax.rsqrt(var + RW_GN_EPS)).reshape(ts, d) * lnx_w + lnx_b
    rh = heads(r)
    bonus = (jnp.sum(rh * heads(kds[0]) * r_k, axis=-1, keepdims=True)
             + jnp.sum(rh * heads(kds[1]) * r_k, axis=-1, keepdims=True))
    return ((yn + (bonus * heads(v)).reshape(ts, d)) * gate).astype(BF16)


def _hgrn2_mixer(u, n_ctx, layer, w_in, lb_logits, norm_w):
    ts, d = u.shape
    cum = jnp.cumsum(jax.nn.softmax(lb_logits, axis=1), axis=1)
    lb = cum[:, layer] - cum[:, 0]
    p = _linear(u, w_in.astype(BF16))
    q, i_in, og, zf_f, zf_b = jnp.split(p, 5, axis=-1)
    q = jax.nn.silu(q)
    ks, gs = [], []
    for dr, zf in enumerate((zf_f, zf_b)):
        ks.append((1.0 - lb[dr]) * jax.nn.sigmoid(-zf))
        gs.append(jnp.logaddexp(jnp.log(lb[dr]), jnp.log1p(-lb[dr]) + jax.nn.log_sigmoid(zf)))
    seq = lambda f, b_: jnp.stack([f, _rev(b_, n_ctx)])
    y = _merge(_gla_scan(seq(q, q), seq(ks[0], ks[1]), seq(i_in, i_in), seq(gs[0], gs[1])), n_ctx)
    return (_rmsnorm(y, norm_w, F32) * jax.nn.silu(og)).astype(BF16)


def _rglru_mixer(u, n_ctx, w_in, conv_w, conv_b, wa, ba, wx, bx, lam):
    log_sig = jax.nn.log_sigmoid(lam)
    p = _linear(u, w_in.astype(BF16))
    gate, xr = jnp.split(p, 2, axis=-1)
    xc = _conv_centred(xr, conv_w, conv_b, n_ctx)
    xcb = xc.astype(BF16)
    a_s, u_s = [], []
    for dr in range(2):
        r = jax.nn.sigmoid(_block_diag(xcb, wa[dr].astype(BF16)) + ba[dr])
        ig = jax.nn.sigmoid(_block_diag(xcb, wx[dr].astype(BF16)) + bx[dr])
        log_a = RG_C * r * log_sig[dr]
        a_s.append(jnp.exp(log_a))
        u_s.append(jnp.sqrt(-jnp.expm1(2.0 * log_a)) * ig * xc)
    seq = lambda f, b_: jnp.stack([f, _rev(b_, n_ctx)])
    h = _merge(_lru_scan(seq(a_s[0], a_s[1]), seq(u_s[0], u_s[1])), n_ctx)
    return (h * jax.nn.gelu(gate)).astype(BF16)


def _mamba2_mixer(u, n_ctx, w_in, conv_w, conv_b, dt_bias, a_log, d_skip, norm_w):
    ts, d = u.shape
    di = 2 * d
    gn = M2_G * M2_N
    hh = di // M2_P
    w_in = w_in.astype(BF16)
    zx = _linear(u, w_in, n_out=di + di + 2 * gn)
    dt_raw = _linear(u, w_in, col0=di + di + 2 * gn, n_out=2 * hh)
    z, xbc = zx[:, :di], zx[:, di:]
    xbc = jax.nn.silu(_conv_centred(xbc, conv_w, conv_b, n_ctx))
    xs, bm, cm = xbc[:, :di], xbc[:, di:di + gn], xbc[:, di + gn:]
    a_neg = -jnp.exp(a_log)
    dt = jax.nn.softplus(dt_raw.reshape(ts, 2, hh) + dt_bias)
    dts = [dt[:, dr] for dr in range(2)]
    das = [dts[dr] * a_neg[dr] for dr in range(2)]
    seq = lambda f, b_: jnp.stack([f, _rev(b_, n_ctx)])
    y = _ssd_scan(seq(xs, xs), seq(bm, bm), seq(cm, cm), seq(dts[0], dts[1]), seq(das[0], das[1]))
    y = _merge(y, n_ctx) + jnp.repeat(d_skip, M2_P) * xs
    y = y * jax.nn.silu(z)
    return _rmsnorm(y, norm_w, BF16, groups=M2_G)


def _moe(v32, vb, router, w_gu, w_down, res, gate, n_ctx):
    e = router.shape[1]
    logits = _linear(v32, router, precision=lax.Precision.HIGHEST)
    top_val, top_idx = lax.top_k(logits, TOP_K)
    gates = jax.nn.softmax(top_val, axis=-1)
    combine = jnp.sum(jax.nn.one_hot(top_idx, e, dtype=F32) * gates[..., None], axis=1)
    act = _moe_up(vb, w_gu.astype(BF16), combine.T[..., None])
    wd = w_down.astype(BF16).reshape(-1, w_down.shape[-1])
    return _linear_residual(act, wd, res, gate, n_ctx)


def kernel(x, c, ctx, c_ctx, mod_down, mod_up, mod_b, norm_w, final_norm_w, rw_mix, rw_w_rkv, rw_w0, rw_w1, rw_w2, rw_a0, rw_a1, rw_a2, rw_g1, rw_g2, rw_k_k, rw_k_a, rw_r_k, rw_lnx_w, rw_lnx_b, rw_w_o, hg_w_in, hg_lb, hg_norm_w, hg_w_o, rg_w_in, rg_conv_w, rg_conv_b, rg_wa, rg_ba, rg_wx, rg_bx, rg_lam, rg_w_out, m2_w_in, m2_conv_w, m2_conv_b, m2_dt_bias, m2_a_log, m2_d, m2_norm_w, m2_w_out, ffn_w_gu, ffn_w_down, moe_router, moe_w_gu, moe_w_down):
    assert x.shape[0] == 1 and c.shape[0] == 1
    depth = mod_down.shape[0]
    n_ctx, d = ctx.shape[1], ctx.shape[2]
    xh = jnp.concatenate([ctx[0], x[0]], axis=0)
    cvec = jnp.zeros((8, d), F32).at[0].set(c_ctx).at[1].set(c[0])
    cvec = jax.nn.silu(cvec).astype(BF16)
    for i in range(depth):
        kind = i % 4
        col_major = i % 2 == 1
        low = _linear(cvec, mod_down[i].astype(BF16)).astype(BF16)
        mod = (_linear(low, mod_up[i].astype(BF16))[:2] + mod_b[i]).reshape(2, 6, d)

        want = (F32,) if kind == 0 else (BF16,)
        u = _to_scan(_norm_mod(xh, norm_w[i, 0], mod, 0, 1, n_ctx, want)[0], n_ctx, col_major)
        if kind == 0:
            o = _rwkv7_mixer(u, n_ctx, rw_mix, rw_w_rkv, rw_w0, rw_w1, rw_w2, rw_a0, rw_a1, rw_a2,
                             rw_g1, rw_g2, rw_k_k, rw_k_a, rw_r_k, rw_lnx_w, rw_lnx_b)
            w_out = rw_w_o
        elif kind == 1:
            o = _hgrn2_mixer(u, n_ctx, i, hg_w_in, hg_lb, hg_norm_w)
            w_out = hg_w_o
        elif kind == 2:
            o = _rglru_mixer(u, n_ctx, rg_w_in, rg_conv_w, rg_conv_b, rg_wa, rg_ba, rg_wx, rg_bx, rg_lam)
            w_out = rg_w_out
        else:
            o = _mamba2_mixer(u, n_ctx, m2_w_in, m2_conv_w, m2_conv_b, m2_dt_bias, m2_a_log, m2_d, m2_norm_w)
            w_out = m2_w_out
        xh = _linear_residual(_to_grid(o, n_ctx, col_major), w_out.astype(BF16), xh, mod[:, 2], n_ctx)

        fi = i // 2
        if i % 2 == 0:
            vb, = _norm_mod(xh, norm_w[i, 1], mod, 4, 3, n_ctx, (BF16,))
            act = _swiglu_up(vb, ffn_w_gu[fi].astype(BF16))
            xh = _linear_residual(act, ffn_w_down[fi].astype(BF16), xh, mod[:, 5], n_ctx)
        else:
            v32, vb = _norm_mod(xh, norm_w[i, 1], mod, 4, 3, n_ctx, (F32, BF16))
            xh = _moe(v32, vb, moe_router[fi], moe_w_gu[fi], moe_w_down[fi], xh, mod[:, 5], n_ctx)
    out = _rmsnorm(xh[n_ctx:], final_norm_w, F32)
    return out[None]
```

```python
import functools
import math

import jax
import jax.numpy as jnp
from jax import lax
from jax.experimental import pallas as pl
from jax.experimental.pallas import tpu as pltpu

F32 = jnp.float32
BF16 = jnp.bfloat16

NORM_EPS = 1e-6
GRID_W = 64
CHUNK = 64
RW_N = 64
RW_GN_EPS = 64e-5
HG_K = 128
RG_H = 16
RG_C = 8.0
M2_P = 64
M2_G = 8
M2_N = 128
TOP_K = 2

_VMEM_LIMIT = 56 * 1024 * 1024

_NN = (((1,), (0,)), ((), ()))
_NT = (((1,), (1,)), ((), ()))
_TN = (((0,), (0,)), ((), ()))


def _params(n_axes):
    return pltpu.CompilerParams(dimension_semantics=("arbitrary",) * n_axes,
                                vmem_limit_bytes=_VMEM_LIMIT)


def _tile(n, prefs):
    for p in prefs:
        if n % p == 0:
            return p
    return n


def _bdot(a, b, dims=_NN):
    return lax.dot_general(a.astype(BF16), b.astype(BF16), dims, preferred_element_type=F32)


def _split3(x):
    hi = x.astype(BF16)
    r1 = x - hi.astype(F32)
    mid = r1.astype(BF16)
    lo = (r1 - mid.astype(F32)).astype(BF16)
    return hi, mid, lo


def _cumsum_rows(tri, x):
    hi, mid, lo = _split3(x)
    d = lambda p: jnp.dot(tri, p, preferred_element_type=F32)
    return d(hi) + d(mid) + d(lo)


def _cumsum_cols(x, triu):
    hi, mid, lo = _split3(x)
    d = lambda p: jnp.dot(p, triu, preferred_element_type=F32)
    return d(hi) + d(mid) + d(lo)


def _scan_masks(c, rev):
    ti = lax.broadcasted_iota(jnp.int32, (c, c), 0)
    si = lax.broadcasted_iota(jnp.int32, (c, c), 1)
    upto = (ti <= si) if rev else (ti >= si)
    before = (ti < si) if rev else (ti > si)
    eye = jnp.where(ti == si, 1.0, 0.0)
    return upto, before, eye, (0 if rev else c - 1)


def _block_order(i, n_first, n_total, rev):
    if not rev:
        return i
    return jnp.where(i < n_first, n_first - 1 - i, n_total + n_first - 1 - i)


def _norm_mod_body(x_ref, w_ref, mod_ref, *o_refs, shift_i, scale_i, n_ctx, tm):
    x = x_ref[...]
    y = x * lax.rsqrt(jnp.mean(x * x, axis=-1, keepdims=True) + NORM_EPS) * w_ref[...]
    row = pl.program_id(0) * tm + lax.broadcasted_iota(jnp.int32, (tm, 1), 0)
    is_ctx = row < n_ctx
    shift = jnp.where(is_ctx, mod_ref[0, shift_i:shift_i + 1, :], mod_ref[1, shift_i:shift_i + 1, :])
    scale = jnp.where(is_ctx, mod_ref[0, scale_i:scale_i + 1, :], mod_ref[1, scale_i:scale_i + 1, :])
    y = y * (1.0 + scale) + shift
    for o in o_refs:
        o[...] = y.astype(o.dtype)


def _norm_mod(xh, w, mod, shift_i, scale_i, n_ctx, dtypes):
    m, d = xh.shape
    tm = _tile(m, (256, 128, 64, 8))
    out = pl.pallas_call(
        functools.partial(_norm_mod_body, shift_i=shift_i, scale_i=scale_i, n_ctx=n_ctx, tm=tm),
        grid=(m // tm,),
        in_specs=[pl.BlockSpec((tm, d), lambda i: (i, 0)),
                  pl.BlockSpec((1, d), lambda i: (0, 0)),
                  pl.BlockSpec((2, 6, d), lambda i: (0, 0, 0))],
        out_specs=[pl.BlockSpec((tm, d), lambda i: (i, 0)) for _ in dtypes],
        out_shape=[jax.ShapeDtypeStruct((m, d), dt) for dt in dtypes],
        compiler_params=_params(1), name="norm_mod",
    )(xh, w.reshape(1, d), mod)
    return out


def _rmsnorm_body(x_ref, w_ref, o_ref, *, groups):
    x = x_ref[...]
    gw = x.shape[-1] // groups
    for g in range(groups):
        xs = x[:, g * gw:(g + 1) * gw]
        o_ref[:, g * gw:(g + 1) * gw] = (
            xs * lax.rsqrt(jnp.mean(xs * xs, axis=-1, keepdims=True) + NORM_EPS)
            * w_ref[:, g * gw:(g + 1) * gw]).astype(o_ref.dtype)


def _rmsnorm(x, w, out_dtype, groups=1):
    m, d = x.shape
    tm = _tile(m, (256, 128, 64, 8))
    return pl.pallas_call(
        functools.partial(_rmsnorm_body, groups=groups),
        grid=(m // tm,),
        in_specs=[pl.BlockSpec((tm, d), lambda i: (i, 0)), pl.BlockSpec((1, d), lambda i: (0, 0))],
        out_specs=pl.BlockSpec((tm, d), lambda i: (i, 0)),
        out_shape=jax.ShapeDtypeStruct((m, d), out_dtype),
        compiler_params=_params(1), name="rmsnorm",
    )(x, w.reshape(1, d))


def _mm_body(*refs, nw, nex, nk, epi, precision):
    x_ref = refs[0]
    w_refs = refs[1:1 + nw]
    ex_refs = refs[1 + nw:1 + nw + nex]
    o_ref = refs[1 + nw + nex]
    acc_refs = refs[2 + nw + nex:]
    x = x_ref[...]
    prods = [jnp.dot(x, w[...], preferred_element_type=F32, precision=precision) for w in w_refs]

    def finish(accs):
        o_ref[...] = epi(accs, [e[...] for e in ex_refs], pl.program_id(1)).astype(o_ref.dtype)

    if nk == 1:
        finish(prods)
    else:
        k = pl.program_id(2)

        @pl.when(k == 0)
        def _():
            for a, p in zip(acc_refs, prods):
                a[...] = p

        @pl.when(k > 0)
        def _():
            for a, p in zip(acc_refs, prods):
                a[...] += p

        @pl.when(k == nk - 1)
        def _():
            finish([a[...] for a in acc_refs])


def _matmul(x, ws, w_specs, *, n_out, epi, tm, tn, tk=None, extras=(), extra_specs=(),
            out_dtype=F32, x_map=None, k_total=None, precision=None, name="mm"):
    m = x.shape[0]
    k_total = k_total or x.shape[1]
    tk = tk or k_total
    nk = k_total // tk
    x_map = x_map or (lambda j, i, k: (i, k))
    return pl.pallas_call(
        functools.partial(_mm_body, nw=len(ws), nex=len(extras), nk=nk, epi=epi, precision=precision),
        grid=(n_out // tn, m // tm, nk),
        in_specs=[pl.BlockSpec((tm, tk), x_map)] + list(w_specs) + list(extra_specs),
        out_specs=pl.BlockSpec((tm, tn), lambda j, i, k: (i, j)),
        out_shape=jax.ShapeDtypeStruct((m, n_out), out_dtype),
        scratch_shapes=[pltpu.VMEM((tm, tn), F32) for _ in ws] if nk > 1 else [],
        compiler_params=_params(3), name=name,
    )(x, *ws, *extras)


def _epi_plain(accs, extras, i):
    return accs[0]


def _row_tile(m):
    return _tile(m, (768, 512, 640, 384, 256, 128, 64, 8))


def _k_tile(k):
    return k if k <= 4096 else _tile(k, (4096, 2048, 1024))


def _linear(x, w, out_dtype=F32, col0=0, n_out=None, precision=None):
    m, k = x.shape
    n_out = n_out or w.shape[1]
    tn = _tile(math.gcd(n_out, col0) if col0 else n_out, (512, 256, 128))
    tn = min(tn, n_out)
    c0 = col0 // tn
    tk = _k_tile(k)
    return _matmul(x, [w], [pl.BlockSpec((tk, tn), lambda j, i, kk: (kk, j + c0))],
                   n_out=n_out, epi=_epi_plain, tm=_row_tile(m), tn=tn, tk=tk,
                   out_dtype=out_dtype, precision=precision)


def _linear_residual(x, w, res, gate, n_ctx):
    m, k = x.shape
    n = w.shape[1]
    tm, tn, tk = _row_tile(m), _tile(n, (512, 256, 128)), _k_tile(k)

    def epi(accs, extras, i):
        r, g = extras
        row = i * tm + lax.broadcasted_iota(jnp.int32, (tm, 1), 0)
        return r + jnp.where(row < n_ctx, g[0:1, :], g[1:2, :]) * accs[0]

    return _matmul(x, [w], [pl.BlockSpec((tk, tn), lambda j, i, kk: (kk, j))],
                   n_out=n, epi=epi, tm=tm, tn=tn, tk=tk, extras=(res, gate),
                   extra_specs=(pl.BlockSpec((tm, tn), lambda j, i, kk: (i, j)),
                                pl.BlockSpec((2, tn), lambda j, i, kk: (0, j))))


def _epi_swiglu(accs, extras, i):
    g, u = accs
    return g * jax.nn.sigmoid(g) * u


def _swiglu_up(x, w_gu):
    m, k = x.shape
    f = w_gu.shape[1] // 2
    tn = _tile(f, (512, 256, 128))
    nj = f // tn
    return _matmul(x, [w_gu, w_gu],
                   [pl.BlockSpec((k, tn), lambda j, i, kk: (0, j)),
                    pl.BlockSpec((k, tn), lambda j, i, kk: (0, j + nj))],
                   n_out=f, epi=_epi_swiglu, tm=_row_tile(m), tn=tn, out_dtype=BF16)


def _moe_up(x, w_gu, comb_t):
    m, k = x.shape
    e, _, f2 = w_gu.shape
    f = f2 // 2
    tn = _tile(f, (512, 256, 128))
    nj = f // tn
    tm = _row_tile(m)

    def epi(accs, extras, i):
        g, u = accs
        return g * jax.nn.sigmoid(g) * u * extras[0]

    return _matmul(x, [w_gu, w_gu],
                   [pl.BlockSpec((None, k, tn), lambda j, i, kk: (j // nj, 0, j % nj)),
                    pl.BlockSpec((None, k, tn), lambda j, i, kk: (j // nj, 0, j % nj + nj))],
                   n_out=e * f, epi=epi, tm=tm, tn=tn, out_dtype=BF16,
                   extras=(comb_t,),
                   extra_specs=(pl.BlockSpec((None, tm, 1), lambda j, i, kk: (j // nj, i, 0)),))


def _block_diag(x, w):
    m = x.shape[0]
    h, bw, _ = w.shape
    return _matmul(x, [w], [pl.BlockSpec((None, bw, bw), lambda j, i, kk: (j, 0, 0))],
                   n_out=h * bw, epi=_epi_plain, tm=_row_tile(m), tn=bw, tk=bw,
                   x_map=lambda j, i, kk: (i, j), k_total=bw)


def _rwkv_body(r_ref, lw_ref, k_ref, v_ref, a_ref, b_ref, o_ref, s_ref, *, heads, n, rev):
    c = CHUNK

    @pl.when(pl.program_id(1) == 0)
    def _():
        s_ref[...] = jnp.zeros_like(s_ref)

    upto, before, eye, row_last = _scan_masks(c, rev)
    tri = jnp.where(upto, 1.0, 0.0).astype(BF16)
    lw = lw_ref[...]
    cum = _cumsum_rows(tri, lw)
    last = cum[row_last:row_last + 1, :]
    g_inv = jnp.exp(-cum)
    tail = jnp.exp(last - cum)
    dec = jnp.exp(last)
    kd, b = k_ref[...], b_ref[...]
    at = (a_ref[...] * jnp.exp(cum - lw)).astype(BF16)
    rt = (r_ref[...] * jnp.exp(cum)).astype(BF16)
    bt, kt = (b * g_inv).astype(BF16), (kd * g_inv).astype(BF16)
    bg, kg = (b * tail).astype(BF16), (kd * tail).astype(BF16)
    vb = v_ref[...].astype(BF16)
    hs = range(heads)
    sl = [slice(h * n, (h + 1) * n) for h in hs]
    lhs = [jnp.concatenate([at[:, s], rt[:, s]], axis=0) for s in sl]
    rhs = [jnp.concatenate([bt[:, s], kt[:, s]], axis=0) for s in sl]
    s0 = [s_ref[h] for h in hs]
    p = [_bdot(lhs[h], rhs[h], _NT) for h in hs]
    lm = [_bdot(lhs[h], s0[h], _NT) for h in hs]
    x = [jnp.where(before, p[h][:c, :c], 0.0) for h in hs]
    t = [eye + xh for xh in x]
    for _ in range(int(math.log2(c)) - 1):
        x = [_bdot(xh, xh) for xh in x]
        t = [th + _bdot(th, xh) for th, xh in zip(t, x)]
    w0 = [lm[h][:c] + _bdot(jnp.where(before, p[h][:c, c:], 0.0), vb[:, sl[h]]) for h in hs]
    u = [_bdot(t[h], w0[h]) for h in hs]
    uv = [jnp.concatenate([u[h].astype(BF16), vb[:, sl[h]]], axis=0) for h in hs]
    for h in hs:
        r_bk = jnp.concatenate([jnp.where(upto, p[h][c:, :c], 0.0), jnp.where(upto, p[h][c:, c:], 0.0)], axis=1)
        o_ref[:, sl[h]] = lm[h][c:] + _bdot(r_bk, uv[h])
    for h in hs:
        bk = jnp.concatenate([bg[:, sl[h]], kg[:, sl[h]]], axis=0)
        s_ref[h] = s0[h] * dec[:, sl[h]] + _bdot(uv[h], bk, _TN)


def _rwkv_scan(r, lw, kd, v, a, b, n_ctx, rev):
    t, d = r.shape
    heads = 16 if d % (16 * RW_N) == 0 else d // RW_N
    w = heads * RW_N
    nch, ncc = t // CHUNK, n_ctx // CHUNK
    spec = pl.BlockSpec((CHUNK, w), lambda g, c: (_block_order(c, ncc, nch, rev), g))
    return pl.pallas_call(
        functools.partial(_rwkv_body, heads=heads, n=RW_N, rev=rev),
        grid=(d // w, nch),
        in_specs=[spec] * 6,
        out_specs=spec,
        out_shape=jax.ShapeDtypeStruct((t, d), F32),
        scratch_shapes=[pltpu.VMEM((heads, RW_N, RW_N), F32)],
        compiler_params=_params(2), name="rwkv_rev" if rev else "rwkv_fwd",
    )(r, lw, kd, v, a, b)


def _gla_body(q_ref, k_ref, v_ref, g_ref, o_ref, s_ref, *, heads, kdim, sub, rev):
    c = CHUNK

    @pl.when(pl.program_id(1) == 0)
    def _():
        s_ref[...] = jnp.zeros_like(s_ref)

    upto, _, _, row_last = _scan_masks(c, rev)
    tri = jnp.where(upto, 1.0, 0.0).astype(BF16)
    cum = _cumsum_rows(tri, g_ref[...])
    last = cum[row_last:row_last + 1, :]
    q, k = q_ref[...], k_ref[...]
    vb = v_ref[...].astype(BF16)
    qe = (q * jnp.exp(cum)).astype(BF16)
    ktail = (k * jnp.exp(last - cum)).astype(BF16)
    dec = jnp.exp(last)
    hs = range(heads)
    sl = [slice(h * kdim, (h + 1) * kdim) for h in hs]
    s0 = [s_ref[h] for h in hs]
    inter = [_bdot(qe[:, sl[h]], s0[h], _NT) for h in hs]
    nb = c // sub
    for i in range(nb):
        lo, hi = i * sub, (i + 1) * sub
        if rev:
            ks = slice(lo, c)
            piv = cum[hi:hi + 1, :] if i < nb - 1 else jnp.zeros_like(last)
        else:
            ks = slice(0, hi)
            piv = cum[lo - 1:lo, :] if i > 0 else jnp.zeros_like(last)
        nk = ks.stop - ks.start
        qi = (q[lo:hi] * jnp.exp(cum[lo:hi] - piv)).astype(BF16)
        ki = (k[ks] * jnp.exp(piv - cum[ks])).astype(BF16)
        tt = lo + lax.broadcasted_iota(jnp.int32, (sub, nk), 0)
        ss = ks.start + lax.broadcasted_iota(jnp.int32, (sub, nk), 1)
        mask = (tt <= ss) if rev else (tt >= ss)
        att = [jnp.where(mask, _bdot(qi[:, sl[h]], ki[:, sl[h]], _NT), 0.0) for h in hs]
        for h in hs:
            o_ref[lo:hi, sl[h]] = inter[h][lo:hi] + _bdot(att[h], vb[ks, sl[h]])
    for h in hs:
        s_ref[h] = s0[h] * dec[:, sl[h]] + _bdot(vb[:, sl[h]], ktail[:, sl[h]], _TN)


def _gla_scan(q, k, v, g, n_ctx, rev):
    t, d = q.shape
    heads = 8 if d % (8 * HG_K) == 0 else d // HG_K
    w = heads * HG_K
    nch, ncc = t // CHUNK, n_ctx // CHUNK
    spec = pl.BlockSpec((CHUNK, w), lambda gi, c: (_block_order(c, ncc, nch, rev), gi))
    return pl.pallas_call(
        functools.partial(_gla_body, heads=heads, kdim=HG_K, sub=16, rev=rev),
        grid=(d // w, nch),
        in_specs=[spec] * 4,
        out_specs=spec,
        out_shape=jax.ShapeDtypeStruct((t, d), F32),
        scratch_shapes=[pltpu.VMEM((heads, HG_K, HG_K), F32)],
        compiler_params=_params(2), name="gla_rev" if rev else "gla_fwd",
    )(q, k, v, g)


def _lru_body(a_ref, u_ref, o_ref, h_ref, *, tb, rev):
    @pl.when(pl.program_id(1) == 0)
    def _():
        h_ref[...] = jnp.zeros_like(h_ref)

    def step(i, h):
        t = tb - 1 - i if rev else i
        h = a_ref[pl.ds(t, 1), :] * h + u_ref[pl.ds(t, 1), :]
        o_ref[pl.ds(t, 1), :] = h
        return h

    h_ref[0:1, :] = lax.fori_loop(0, tb, step, h_ref[0:1, :], unroll=8)


def _lru_scan(a, u, n_ctx, rev):
    t, w = a.shape
    tb = _tile(math.gcd(t, n_ctx), (256, 128, 64, 8))
    lb = _tile(w, (512, 256, 128))
    nb, nbc = t // tb, n_ctx // tb
    spec = pl.BlockSpec((tb, lb), lambda j, i: (_block_order(i, nbc, nb, rev), j))
    return pl.pallas_call(
        functools.partial(_lru_body, tb=tb, rev=rev),
        grid=(w // lb, nb),
        in_specs=[spec, spec],
        out_specs=spec,
        out_shape=jax.ShapeDtypeStruct((t, w), F32),
        scratch_shapes=[pltpu.VMEM((8, lb), F32)],
        compiler_params=_params(2), name="lru_rev" if rev else "lru_fwd",
    )(a, u)


def _ssd_body(x_ref, b_ref, c_ref, dtc_ref, dac_ref, dtr_ref, dar_ref, o_ref, s_ref, xw_ref, dec_ref, *, r, p, rev):
    c = CHUNK

    @pl.when(pl.program_id(1) == 0)
    def _():
        s_ref[...] = jnp.zeros_like(s_ref)

    upto, _, _, row_last = _scan_masks(c, rev)
    tri = jnp.where(upto, 1.0, 0.0).astype(BF16)
    tri_t = jnp.where(_scan_masks(c, not rev)[0], 1.0, 0.0).astype(BF16)
    cum_col = _cumsum_rows(tri, dac_ref[...])
    cum_row = _cumsum_cols(dar_ref[...], tri_t)
    dt_col = dtc_ref[...]
    dt_row = dtr_ref[...]
    bm, cm = b_ref[...], c_ref[...]
    cb = _bdot(cm, bm, _NT)
    cs = _bdot(cm, s_ref[...])
    hs = range(r)
    sl = [slice(h * p, (h + 1) * p) for h in hs]
    last = cum_col[row_last:row_last + 1, :]
    e_in = jnp.exp(cum_col)
    wcol = jnp.exp(last - cum_col) * dt_col
    e_last = jnp.exp(last)
    xb = x_ref[...]
    m = [cb * jnp.exp(jnp.where(upto, cum_col[:, h:h + 1] - cum_row[h:h + 1, :], -jnp.inf)) * dt_row[h:h + 1, :]
         for h in hs]
    y = [_bdot(m[h], xb[:, sl[h]]) for h in hs]
    for h in hs:
        o_ref[:, sl[h]] = y[h] + cs[:, sl[h]] * e_in[:, h:h + 1]
        xw_ref[:, sl[h]] = (xb[:, sl[h]] * wcol[:, h:h + 1]).astype(BF16)
        dec_ref[:, sl[h]] = jnp.broadcast_to(e_last[:, h:h + 1], (8, p))
    s_ref[...] = s_ref[...] * dec_ref[0:1, :] + _bdot(bm, xw_ref[...], _TN)


def _ssd_scan(x, bm, cm, dt, da, n_ctx, rev):
    t, di = x.shape
    g = M2_G
    r = di // (g * M2_P)
    gw = r * M2_P
    nch, ncc = t // CHUNK, n_ctx // CHUNK
    col = lambda z: z.reshape(t, g, r).transpose(1, 0, 2)
    row = lambda z: z.reshape(nch, CHUNK, g, r).transpose(2, 0, 3, 1)
    order = lambda c: _block_order(c, ncc, nch, rev)
    xspec = pl.BlockSpec((CHUNK, gw), lambda gi, c: (order(c), gi))
    nspec = pl.BlockSpec((CHUNK, M2_N), lambda gi, c: (order(c), gi))
    cspec = pl.BlockSpec((None, CHUNK, r), lambda gi, c: (gi, order(c), 0))
    rspec = pl.BlockSpec((None, None, r, CHUNK), lambda gi, c: (gi, order(c), 0, 0))
    return pl.pallas_call(
        functools.partial(_ssd_body, r=r, p=M2_P, rev=rev),
        grid=(g, nch),
        in_specs=[xspec, nspec, nspec, cspec, cspec, rspec, rspec],
        out_specs=xspec,
        out_shape=jax.ShapeDtypeStruct((t, di), F32),
        scratch_shapes=[pltpu.VMEM((M2_N, gw), F32), pltpu.VMEM((CHUNK, gw), BF16), pltpu.VMEM((8, gw), F32)],
        compiler_params=_params(2), name="ssd_rev" if rev else "ssd_fwd",
    )(x, bm, cm, col(dt), col(da), row(dt), row(da))


def _to_scan(x, n_ctx, column_major):
    if not column_major:
        return x
    lat = x[n_ctx:]
    t, d = lat.shape
    lat = lat.reshape(t // GRID_W, GRID_W, d).transpose(1, 0, 2).reshape(t, d)
    return jnp.concatenate([x[:n_ctx], lat], axis=0)


def _to_grid(x, n_ctx, column_major):
    if not column_major:
        return x
    lat = x[n_ctx:]
    t, d = lat.shape
    lat = lat.reshape(GRID_W, t // GRID_W, d).transpose(1, 0, 2).reshape(t, d)
    return jnp.concatenate([x[:n_ctx], lat], axis=0)


def _shifted(x, n_ctx, off):
    ts = x.shape[0]
    t = jnp.arange(ts)[:, None]
    lo = jnp.where(t < n_ctx, 0, n_ctx)
    hi = jnp.where(t < n_ctx, n_ctx, ts)
    ok = (t + off >= lo) & (t + off < hi)
    return jnp.where(ok, jnp.roll(x, -off, axis=0), 0.0)


def _conv_centred(x, w, b, n_ctx):
    return sum(w[j] * _shifted(x, n_ctx, j - 2) for j in range(w.shape[0])) + b


def _rwkv7_mixer(u, n_ctx, mix, w_rkv, w0, w1, w2, a0, a1, a2, g1, g2, k_k, k_a, r_k, lnx_w, lnx_b):
    ts, d = u.shape
    hn = d // RW_N
    xx = 0.5 * (_shifted(u, n_ctx, -1) + _shifted(u, n_ctx, 1)) - u
    xs = [(u + xx * mix[j]).astype(BF16) for j in range(6)]
    r, k, v = (_linear(xs[j], w_rkv[j].astype(BF16)) for j in range(3))
    gate = _linear(jax.nn.sigmoid(_linear(xs[5], g1.astype(BF16))).astype(BF16), g2.astype(BF16))
    heads = lambda z: z.reshape(ts, hn, RW_N)
    kk = heads(k * k_k)
    kk = (kk * lax.rsqrt(jnp.sum(kk * kk, axis=-1, keepdims=True) + 1e-12)).reshape(ts, d)
    kds, ys = [], []
    for dr in range(2):
        z = w0[dr] + _linear(jnp.tanh(_linear(xs[3], w1[dr].astype(BF16))).astype(BF16), w2[dr].astype(BF16))
        lw = -math.exp(-0.5) * jax.nn.sigmoid(z)
        a = jax.nn.sigmoid(a0[dr] + _linear(_linear(xs[4], a1[dr].astype(BF16)).astype(BF16), a2[dr].astype(BF16)))
        kds.append(k * (1.0 + (a - 1.0) * k_a))
        ys.append(_rwkv_scan(r, lw, kds[dr], v, -kk, kk * a, n_ctx, dr == 1))
    y = heads(ys[0] + ys[1])
    mu = jnp.mean(y, axis=-1, keepdims=True)
    var = jnp.mean(jnp.square(y - mu), axis=-1, keepdims=True)
    yn = ((y - mu) * lax.rsqrt(var + RW_GN_EPS)).reshape(ts, d) * lnx_w + lnx_b
    rh = heads(r)
    bonus = (jnp.sum(rh * heads(kds[0]) * r_k, axis=-1, keepdims=True)
             + jnp.sum(rh * heads(kds[1]) * r_k, axis=-1, keepdims=True))
    return ((yn + (bonus * heads(v)).reshape(ts, d)) * gate).astype(BF16)


def _hgrn2_mixer(u, n_ctx, layer, w_in, lb_logits, norm_w):
    cum = jnp.cumsum(jax.nn.softmax(lb_logits, axis=1), axis=1)
    lb = cum[:, layer] - cum[:, 0]
    p = _linear(u, w_in.astype(BF16))
    q, i_in, og, zf_f, zf_b = jnp.split(p, 5, axis=-1)
    q = jax.nn.silu(q)
    ys = []
    for dr, zf in enumerate((zf_f, zf_b)):
        k = (1.0 - lb[dr]) * jax.nn.sigmoid(-zf)
        g = jnp.logaddexp(jnp.log(lb[dr]), jnp.log1p(-lb[dr]) + jax.nn.log_sigmoid(zf))
        ys.append(_gla_scan(q, k, i_in, g, n_ctx, dr == 1))
    return (_rmsnorm(ys[0] + ys[1], norm_w, F32) * jax.nn.silu(og)).astype(BF16)


def _rglru_mixer(u, n_ctx, w_in, conv_w, conv_b, wa, ba, wx, bx, lam):
    log_sig = jax.nn.log_sigmoid(lam)
    p = _linear(u, w_in.astype(BF16))
    gate, xr = jnp.split(p, 2, axis=-1)
    xc = _conv_centred(xr, conv_w, conv_b, n_ctx)
    xcb = xc.astype(BF16)
    hs = []
    for dr in range(2):
        r = jax.nn.sigmoid(_block_diag(xcb, wa[dr].astype(BF16)) + ba[dr])
        ig = jax.nn.sigmoid(_block_diag(xcb, wx[dr].astype(BF16)) + bx[dr])
        log_a = RG_C * r * log_sig[dr]
        hs.append(_lru_scan(jnp.exp(log_a), jnp.sqrt(-jnp.expm1(2.0 * log_a)) * ig * xc, n_ctx, dr == 1))
    return ((hs[0] + hs[1]) * jax.nn.gelu(gate)).astype(BF16)


def _mamba2_mixer(u, n_ctx, w_in, conv_w, conv_b, dt_bias, a_log, d_skip, norm_w):
    ts, d = u.shape
    di = 2 * d
    gn = M2_G * M2_N
    hh = di // M2_P
    w_in = w_in.astype(BF16)
    zx = _linear(u, w_in, n_out=di + di + 2 * gn)
    dt_raw = _linear(u, w_in, col0=di + di + 2 * gn, n_out=2 * hh)
    z, xbc = zx[:, :di], zx[:, di:]
    xbc = jax.nn.silu(_conv_centred(xbc, conv_w, conv_b, n_ctx))
    xs, bm, cm = xbc[:, :di], xbc[:, di:di + gn], xbc[:, di + gn:]
    a_neg = -jnp.exp(a_log)
    dt = jax.nn.softplus(dt_raw.reshape(ts, 2, hh) + dt_bias)
    ys = [_ssd_scan(xs, bm, cm, dt[:, dr], dt[:, dr] * a_neg[dr], n_ctx, dr == 1) for dr in range(2)]
    y = ys[0] + ys[1] + jnp.repeat(d_skip, M2_P) * xs
    y = y * jax.nn.silu(z)
    return _rmsnorm(y, norm_w, BF16, groups=M2_G)


def _moe(v32, vb, router, w_gu, w_down, res, gate, n_ctx):
    e = router.shape[1]
    logits = _linear(v32, router, precision=lax.Precision.HIGHEST)
    top_val, top_idx = lax.top_k(logits, TOP_K)
    gates = jax.nn.softmax(top_val, axis=-1)
    combine = jnp.sum(jax.nn.one_hot(top_idx, e, dtype=F32) * gates[..., None], axis=1)
    act = _moe_up(vb, w_gu.astype(BF16), combine.T[..., None])
    wd = w_down.astype(BF16).reshape(-1, w_down.shape[-1])
    return _linear_residual(act, wd, res, gate, n_ctx)


def kernel(x, c, ctx, c_ctx, mod_down, mod_up, mod_b, norm_w, final_norm_w, rw_mix, rw_w_rkv, rw_w0, rw_w1, rw_w2, rw_a0, rw_a1, rw_a2, rw_g1, rw_g2, rw_k_k, rw_k_a, rw_r_k, rw_lnx_w, rw_lnx_b, rw_w_o, hg_w_in, hg_lb, hg_norm_w, hg_w_o, rg_w_in, rg_conv_w, rg_conv_b, rg_wa, rg_ba, rg_wx, rg_bx, rg_lam, rg_w_out, m2_w_in, m2_conv_w, m2_conv_b, m2_dt_bias, m2_a_log, m2_d, m2_norm_w, m2_w_out, ffn_w_gu, ffn_w_down, moe_router, moe_w_gu, moe_w_down):
    assert x.shape[0] == 1 and c.shape[0] == 1
    depth = mod_down.shape[0]
    n_ctx, d = ctx.shape[1], ctx.shape[2]
    xh = jnp.concatenate([ctx[0], x[0]], axis=0)
    cvec = jnp.zeros((8, d), F32).at[0].set(c_ctx).at[1].set(c[0])
    cvec = jax.nn.silu(cvec).astype(BF16)
    for i in range(depth):
        kind = i % 4
        col_major = i % 2 == 1
        low = _linear(cvec, mod_down[i].astype(BF16)).astype(BF16)
        mod = (_linear(low, mod_up[i].astype(BF16))[:2] + mod_b[i]).reshape(2, 6, d)

        want = (F32,) if kind == 0 else (BF16,)
        u = _to_scan(_norm_mod(xh, norm_w[i, 0], mod, 0, 1, n_ctx, want)[0], n_ctx, col_major)
        if kind == 0:
            o = _rwkv7_mixer(u, n_ctx, rw_mix, rw_w_rkv, rw_w0, rw_w1, rw_w2, rw_a0, rw_a1, rw_a2,
                             rw_g1, rw_g2, rw_k_k, rw_k_a, rw_r_k, rw_lnx_w, rw_lnx_b)
            w_out = rw_w_o
        elif kind == 1:
            o = _hgrn2_mixer(u, n_ctx, i, hg_w_in, hg_lb, hg_norm_w)
            w_out = hg_w_o
        elif kind == 2:
            o = _rglru_mixer(u, n_ctx, rg_w_in, rg_conv_w, rg_conv_b, rg_wa, rg_ba, rg_wx, rg_bx, rg_lam)
            w_out = rg_w_out
        else:
            o = _mamba2_mixer(u, n_ctx, m2_w_in, m2_conv_w, m2_conv_b, m2_dt_bias, m2_a_log, m2_d, m2_norm_w)
            w_out = m2_w_out
        xh = _linear_residual(_to_grid(o, n_ctx, col_major), w_out.astype(BF16), xh, mod[:, 2], n_ctx)

        fi = i // 2
        if i % 2 == 0:
            vb, = _norm_mod(xh, norm_w[i, 1], mod, 4, 3, n_ctx, (BF16,))
            act = _swiglu_up(vb, ffn_w_gu[fi].astype(BF16))
            xh = _linear_residual(act, ffn_w_down[fi].astype(BF16), xh, mod[:, 5], n_ctx)
        else:
            v32, vb = _norm_mod(xh, norm_w[i, 1], mod, 4, 3, n_ctx, (F32, BF16))
            xh = _moe(v32, vb, moe_router[fi], moe_w_gu[fi], moe_w_down[fi], xh, mod[:, 5], n_ctx)
    out = _rmsnorm(xh[n_ctx:], final_norm_w, F32)
    return out[None]
```

```python
import functools
import math

import jax
import jax.numpy as jnp
from jax import lax
from jax.experimental import pallas as pl
from jax.experimental.pallas import tpu as pltpu

F32 = jnp.float32
BF16 = jnp.bfloat16

NORM_EPS = 1e-6
GRID_W = 64
CHUNK = 64
RW_N = 64
RW_GN_EPS = 64e-5
HG_K = 128
RG_H = 16
RG_C = 8.0
M2_P = 64
M2_G = 8
M2_N = 128
TOP_K = 2

_VMEM_LIMIT = 56 * 1024 * 1024

_NN = (((1,), (0,)), ((), ()))
_NT = (((1,), (1,)), ((), ()))
_TN = (((0,), (0,)), ((), ()))


def _params(n_axes):
    return pltpu.CompilerParams(dimension_semantics=("arbitrary",) * n_axes,
                                vmem_limit_bytes=_VMEM_LIMIT)


def _tile(n, prefs):
    for p in prefs:
        if n % p == 0:
            return p
    return n


def _bdot(a, b, dims=_NN):
    return lax.dot_general(a.astype(BF16), b.astype(BF16), dims, preferred_element_type=F32)


def _split3(x):
    hi = x.astype(BF16)
    r1 = x - hi.astype(F32)
    mid = r1.astype(BF16)
    lo = (r1 - mid.astype(F32)).astype(BF16)
    return hi, mid, lo


def _cumsum_rows(tri, x):
    hi, mid, lo = _split3(x)
    d = lambda p: jnp.dot(tri, p, preferred_element_type=F32)
    return d(hi) + d(mid) + d(lo)


def _cumsum_cols(x, triu):
    hi, mid, lo = _split3(x)
    d = lambda p: jnp.dot(p, triu, preferred_element_type=F32)
    return d(hi) + d(mid) + d(lo)


def _scan_masks(c, rev):
    ti = lax.broadcasted_iota(jnp.int32, (c, c), 0)
    si = lax.broadcasted_iota(jnp.int32, (c, c), 1)
    upto = (ti <= si) if rev else (ti >= si)
    before = (ti < si) if rev else (ti > si)
    eye = jnp.where(ti == si, 1.0, 0.0)
    return upto, before, eye, (0 if rev else c - 1)


def _block_order(i, n_first, n_total, rev):
    if not rev:
        return i
    return jnp.where(i < n_first, n_first - 1 - i, n_total + n_first - 1 - i)


def _norm_mod_body(x_ref, w_ref, mod_ref, *o_refs, shift_i, scale_i, n_ctx, tm):
    x = x_ref[...]
    y = x * lax.rsqrt(jnp.mean(x * x, axis=-1, keepdims=True) + NORM_EPS) * w_ref[...]
    row = pl.program_id(0) * tm + lax.broadcasted_iota(jnp.int32, (tm, 1), 0)
    is_ctx = row < n_ctx
    shift = jnp.where(is_ctx, mod_ref[0, shift_i:shift_i + 1, :], mod_ref[1, shift_i:shift_i + 1, :])
    scale = jnp.where(is_ctx, mod_ref[0, scale_i:scale_i + 1, :], mod_ref[1, scale_i:scale_i + 1, :])
    y = y * (1.0 + scale) + shift
    for o in o_refs:
        o[...] = y.astype(o.dtype)


def _norm_mod(xh, w, mod, shift_i, scale_i, n_ctx, dtypes):
    m, d = xh.shape
    tm = _tile(m, (256, 128, 64, 8))
    out = pl.pallas_call(
        functools.partial(_norm_mod_body, shift_i=shift_i, scale_i=scale_i, n_ctx=n_ctx, tm=tm),
        grid=(m // tm,),
        in_specs=[pl.BlockSpec((tm, d), lambda i: (i, 0)),
                  pl.BlockSpec((1, d), lambda i: (0, 0)),
                  pl.BlockSpec((2, 6, d), lambda i: (0, 0, 0))],
        out_specs=[pl.BlockSpec((tm, d), lambda i: (i, 0)) for _ in dtypes],
        out_shape=[jax.ShapeDtypeStruct((m, d), dt) for dt in dtypes],
        compiler_params=_params(1), name="norm_mod",
    )(xh, w.reshape(1, d), mod)
    return out


def _rmsnorm_body(x_ref, w_ref, o_ref, *, groups):
    x = x_ref[...]
    gw = x.shape[-1] // groups
    for g in range(groups):
        xs = x[:, g * gw:(g + 1) * gw]
        o_ref[:, g * gw:(g + 1) * gw] = (
            xs * lax.rsqrt(jnp.mean(xs * xs, axis=-1, keepdims=True) + NORM_EPS)
            * w_ref[:, g * gw:(g + 1) * gw]).astype(o_ref.dtype)


def _rmsnorm(x, w, out_dtype, groups=1):
    m, d = x.shape
    tm = _tile(m, (256, 128, 64, 8))
    return pl.pallas_call(
        functools.partial(_rmsnorm_body, groups=groups),
        grid=(m // tm,),
        in_specs=[pl.BlockSpec((tm, d), lambda i: (i, 0)), pl.BlockSpec((1, d), lambda i: (0, 0))],
        out_specs=pl.BlockSpec((tm, d), lambda i: (i, 0)),
        out_shape=jax.ShapeDtypeStruct((m, d), out_dtype),
        compiler_params=_params(1), name="rmsnorm",
    )(x, w.reshape(1, d))


def _mm_body(*refs, nw, nex, nk, epi, precision):
    x_ref = refs[0]
    w_refs = refs[1:1 + nw]
    ex_refs = refs[1 + nw:1 + nw + nex]
    o_ref = refs[1 + nw + nex]
    acc_refs = refs[2 + nw + nex:]
    x = x_ref[...]
    prods = [jnp.dot(x, w[...] if precision is not None else w[...].astype(x.dtype),
                     preferred_element_type=F32, precision=precision) for w in w_refs]

    def finish(accs):
        o_ref[...] = epi(accs, [e[...] for e in ex_refs], pl.program_id(1)).astype(o_ref.dtype)

    if nk == 1:
        finish(prods)
    else:
        k = pl.program_id(2)

        @pl.when(k == 0)
        def _():
            for a, p in zip(acc_refs, prods):
                a[...] = p

        @pl.when(k > 0)
        def _():
            for a, p in zip(acc_refs, prods):
                a[...] += p

        @pl.when(k == nk - 1)
        def _():
            finish([a[...] for a in acc_refs])


def _matmul(x, ws, w_specs, *, n_out, epi, tm, tn, tk=None, extras=(), extra_specs=(),
            out_dtype=F32, x_map=None, k_total=None, precision=None, name="mm"):
    m = x.shape[0]
    k_total = k_total or x.shape[1]
    tk = tk or k_total
    nk = k_total // tk
    x_map = x_map or (lambda j, i, k: (i, k))
    return pl.pallas_call(
        functools.partial(_mm_body, nw=len(ws), nex=len(extras), nk=nk, epi=epi, precision=precision),
        grid=(n_out // tn, m // tm, nk),
        in_specs=[pl.BlockSpec((tm, tk), x_map)] + list(w_specs) + list(extra_specs),
        out_specs=pl.BlockSpec((tm, tn), lambda j, i, k: (i, j)),
        out_shape=jax.ShapeDtypeStruct((m, n_out), out_dtype),
        scratch_shapes=[pltpu.VMEM((tm, tn), F32) for _ in ws] if nk > 1 else [],
        compiler_params=_params(3), name=name,
    )(x, *ws, *extras)


def _wt(w):
    return w if w.shape[-2] <= 4096 else w.astype(BF16)


def _epi_plain(accs, extras, i):
    return accs[0]


def _row_tile(m):
    return _tile(m, (768, 512, 640, 384, 256, 128, 64, 8))


def _k_tile(k):
    return k if k <= 4096 else _tile(k, (4096, 2048, 1024))


def _linear(x, w, out_dtype=F32, col0=0, n_out=None, precision=None):
    m, k = x.shape
    n_out = n_out or w.shape[1]
    tn = _tile(math.gcd(n_out, col0) if col0 else n_out, (512, 256, 128))
    tn = min(tn, n_out)
    c0 = col0 // tn
    tk = _k_tile(k)
    return _matmul(x, [w], [pl.BlockSpec((tk, tn), lambda j, i, kk: (kk, j + c0))],
                   n_out=n_out, epi=_epi_plain, tm=_row_tile(m), tn=tn, tk=tk,
                   out_dtype=out_dtype, precision=precision)


def _linear_residual(x, w, res, gate, n_ctx):
    m, k = x.shape
    n = w.shape[1]
    tm, tn, tk = _row_tile(m), _tile(n, (512, 256, 128)), _k_tile(k)

    def epi(accs, extras, i):
        r, g = extras
        row = i * tm + lax.broadcasted_iota(jnp.int32, (tm, 1), 0)
        return r + jnp.where(row < n_ctx, g[0:1, :], g[1:2, :]) * accs[0]

    return _matmul(x, [w], [pl.BlockSpec((tk, tn), lambda j, i, kk: (kk, j))],
                   n_out=n, epi=epi, tm=tm, tn=tn, tk=tk, extras=(res, gate),
                   extra_specs=(pl.BlockSpec((tm, tn), lambda j, i, kk: (i, j)),
                                pl.BlockSpec((2, tn), lambda j, i, kk: (0, j))))


def _epi_swiglu(accs, extras, i):
    g, u = accs
    return g * jax.nn.sigmoid(g) * u


def _swiglu_up(x, w_gu):
    m, k = x.shape
    f = w_gu.shape[1] // 2
    tn = _tile(f, (512, 256, 128))
    nj = f // tn
    return _matmul(x, [w_gu, w_gu],
                   [pl.BlockSpec((k, tn), lambda j, i, kk: (0, j)),
                    pl.BlockSpec((k, tn), lambda j, i, kk: (0, j + nj))],
                   n_out=f, epi=_epi_swiglu, tm=_row_tile(m), tn=tn, out_dtype=BF16)


def _moe_up(x, w_gu, comb_t):
    m, k = x.shape
    e, _, f2 = w_gu.shape
    f = f2 // 2
    tn = _tile(f, (512, 256, 128))
    nj = f // tn
    tm = _row_tile(m)

    def epi(accs, extras, i):
        g, u = accs
        return g * jax.nn.sigmoid(g) * u * extras[0]

    return _matmul(x, [w_gu, w_gu],
                   [pl.BlockSpec((None, k, tn), lambda j, i, kk: (j // nj, 0, j % nj)),
                    pl.BlockSpec((None, k, tn), lambda j, i, kk: (j // nj, 0, j % nj + nj))],
                   n_out=e * f, epi=epi, tm=tm, tn=tn, out_dtype=BF16,
                   extras=(comb_t,),
                   extra_specs=(pl.BlockSpec((None, tm, 1), lambda j, i, kk: (j // nj, i, 0)),))


def _block_diag(x, w):
    m = x.shape[0]
    h, bw, _ = w.shape
    return _matmul(x, [w], [pl.BlockSpec((None, bw, bw), lambda j, i, kk: (j, 0, 0))],
                   n_out=h * bw, epi=_epi_plain, tm=_row_tile(m), tn=bw, tk=bw,
                   x_map=lambda j, i, kk: (i, j), k_total=bw)


def _rwkv_body(r_ref, k_ref, v_ref, kk_ref, zw_ref, za_ref, w0_ref, a0_ref, ka_ref, o_ref, s_ref, *, heads, n, rev):
    c = CHUNK

    @pl.when(pl.program_id(1) == 0)
    def _():
        s_ref[...] = jnp.zeros_like(s_ref)

    upto, before, eye, row_last = _scan_masks(c, rev)
    tri = jnp.where(upto, 1.0, 0.0).astype(BF16)
    lw = -math.exp(-0.5) * jax.nn.sigmoid(w0_ref[...] + zw_ref[...])
    a_gate = jax.nn.sigmoid(a0_ref[...] + za_ref[...])
    kk = kk_ref[...]
    kd = k_ref[...] * (1.0 + (a_gate - 1.0) * ka_ref[...])
    b = kk * a_gate
    cum = _cumsum_rows(tri, lw)
    last = cum[row_last:row_last + 1, :]
    g_inv = jnp.exp(-cum)
    tail = jnp.exp(last - cum)
    dec = jnp.exp(last)
    at = (-kk * jnp.exp(cum - lw)).astype(BF16)
    rt = (r_ref[...] * jnp.exp(cum)).astype(BF16)
    bt, kt = (b * g_inv).astype(BF16), (kd * g_inv).astype(BF16)
    bg, kg = (b * tail).astype(BF16), (kd * tail).astype(BF16)
    vb = v_ref[...].astype(BF16)
    hs = range(heads)
    sl = [slice(h * n, (h + 1) * n) for h in hs]
    lhs = [jnp.concatenate([at[:, s], rt[:, s]], axis=0) for s in sl]
    rhs = [jnp.concatenate([bt[:, s], kt[:, s]], axis=0) for s in sl]
    s0 = [s_ref[h] for h in hs]
    p = [_bdot(lhs[h], rhs[h], _NT) for h in hs]
    lm = [_bdot(lhs[h], s0[h], _NT) for h in hs]
    x = [jnp.where(before, p[h][:c, :c], 0.0) for h in hs]
    t = [eye + xh for xh in x]
    for _ in range(int(math.log2(c)) - 1):
        x = [_bdot(xh, xh) for xh in x]
        t = [th + _bdot(th, xh) for th, xh in zip(t, x)]
    w0 = [lm[h][:c] + _bdot(jnp.where(before, p[h][:c, c:], 0.0), vb[:, sl[h]]) for h in hs]
    u = [_bdot(t[h], w0[h]) for h in hs]
    uv = [jnp.concatenate([u[h].astype(BF16), vb[:, sl[h]]], axis=0) for h in hs]
    for h in hs:
        r_bk = jnp.concatenate([jnp.where(upto, p[h][c:, :c], 0.0), jnp.where(upto, p[h][c:, c:], 0.0)], axis=1)
        o_ref[:, sl[h]] = lm[h][c:] + _bdot(r_bk, uv[h])
    for h in hs:
        bk = jnp.concatenate([bg[:, sl[h]], kg[:, sl[h]]], axis=0)
        s_ref[h] = s0[h] * dec[:, sl[h]] + _bdot(uv[h], bk, _TN)


def _rwkv_scan(r, k, v, kk, zw, za, w0, a0, k_a, n_ctx, rev):
    t, d = r.shape
    heads = 16 if d % (16 * RW_N) == 0 else d // RW_N
    w = heads * RW_N
    nch, ncc = t // CHUNK, n_ctx // CHUNK
    spec = pl.BlockSpec((CHUNK, w), lambda g, c: (_block_order(c, ncc, nch, rev), g))
    pspec = pl.BlockSpec((1, w), lambda g, c: (0, g))
    return pl.pallas_call(
        functools.partial(_rwkv_body, heads=heads, n=RW_N, rev=rev),
        grid=(d // w, nch),
        in_specs=[spec] * 6 + [pspec] * 3,
        out_specs=spec,
        out_shape=jax.ShapeDtypeStruct((t, d), F32),
        scratch_shapes=[pltpu.VMEM((heads, RW_N, RW_N), F32)],
        compiler_params=_params(2), name="rwkv_rev" if rev else "rwkv_fwd",
    )(r, k, v, kk, zw, za, w0.reshape(1, d), a0.reshape(1, d), k_a.reshape(1, d))


def _gla_body(q_ref, zf_ref, v_ref, lb_ref, o_ref, s_ref, *, heads, kdim, sub, rev):
    c = CHUNK

    @pl.when(pl.program_id(1) == 0)
    def _():
        s_ref[...] = jnp.zeros_like(s_ref)

    upto, _, _, row_last = _scan_masks(c, rev)
    tri = jnp.where(upto, 1.0, 0.0).astype(BF16)
    lb, zf = lb_ref[...], zf_ref[...]
    cum = _cumsum_rows(tri, jnp.log(lb + (1.0 - lb) * jax.nn.sigmoid(zf)))
    last = cum[row_last:row_last + 1, :]
    k = (1.0 - lb) * jax.nn.sigmoid(-zf)
    q = q_ref[...]
    q = q * jax.nn.sigmoid(q)
    vb = v_ref[...].astype(BF16)
    qe = (q * jnp.exp(cum)).astype(BF16)
    ktail = (k * jnp.exp(last - cum)).astype(BF16)
    dec = jnp.exp(last)
    hs = range(heads)
    sl = [slice(h * kdim, (h + 1) * kdim) for h in hs]
    s0 = [s_ref[h] for h in hs]
    inter = [_bdot(qe[:, sl[h]], s0[h], _NT) for h in hs]
    nb = c // sub
    for i in range(nb):
        lo, hi = i * sub, (i + 1) * sub
        if rev:
            ks = slice(lo, c)
            piv = cum[hi:hi + 1, :] if i < nb - 1 else jnp.zeros_like(last)
        else:
            ks = slice(0, hi)
            piv = cum[lo - 1:lo, :] if i > 0 else jnp.zeros_like(last)
        nk = ks.stop - ks.start
        qi = (q[lo:hi] * jnp.exp(cum[lo:hi] - piv)).astype(BF16)
        ki = (k[ks] * jnp.exp(piv - cum[ks])).astype(BF16)
        tt = lo + lax.broadcasted_iota(jnp.int32, (sub, nk), 0)
        ss = ks.start + lax.broadcasted_iota(jnp.int32, (sub, nk), 1)
        mask = (tt <= ss) if rev else (tt >= ss)
        att = [jnp.where(mask, _bdot(qi[:, sl[h]], ki[:, sl[h]], _NT), 0.0) for h in hs]
        for h in hs:
            o_ref[lo:hi, sl[h]] = inter[h][lo:hi] + _bdot(att[h], vb[ks, sl[h]])
    for h in hs:
        s_ref[h] = s0[h] * dec[:, sl[h]] + _bdot(vb[:, sl[h]], ktail[:, sl[h]], _TN)


def _gla_scan(p, lb, d, n_ctx, rev):
    t = p.shape[0]
    heads = 8 if d % (8 * HG_K) == 0 else d // HG_K
    w = heads * HG_K
    nd = d // w
    nch, ncc = t // CHUNK, n_ctx // CHUNK
    zf_col = (4 if rev else 3) * nd

    def spec(col):
        return pl.BlockSpec((CHUNK, w), lambda gi, c: (_block_order(c, ncc, nch, rev), col + gi))

    return pl.pallas_call(
        functools.partial(_gla_body, heads=heads, kdim=HG_K, sub=16, rev=rev),
        grid=(nd, nch),
        in_specs=[spec(0), spec(zf_col), spec(nd), pl.BlockSpec((1, w), lambda gi, c: (0, gi))],
        out_specs=spec(0),
        out_shape=jax.ShapeDtypeStruct((t, d), F32),
        scratch_shapes=[pltpu.VMEM((heads, HG_K, HG_K), F32)],
        compiler_params=_params(2), name="gla_rev" if rev else "gla_fwd",
    )(p, p, p, lb.reshape(1, d))


def _gla_post_body(yf_ref, yr_ref, og_ref, w_ref, o_ref):
    y = yf_ref[...] + yr_ref[...]
    og = og_ref[...]
    y = y * lax.rsqrt(jnp.mean(y * y, axis=-1, keepdims=True) + NORM_EPS) * w_ref[...]
    o_ref[...] = (y * (og * jax.nn.sigmoid(og))).astype(o_ref.dtype)


def _gla_post(y_f, y_r, p, norm_w):
    t, d = y_f.shape
    tm = _tile(t, (256, 128, 64, 8))
    row = pl.BlockSpec((tm, d), lambda i: (i, 0))
    return pl.pallas_call(
        _gla_post_body,
        grid=(t // tm,),
        in_specs=[row, row, pl.BlockSpec((tm, d), lambda i: (i, 2)), pl.BlockSpec((1, d), lambda i: (0, 0))],
        out_specs=row,
        out_shape=jax.ShapeDtypeStruct((t, d), BF16),
        compiler_params=_params(1), name="gla_post",
    )(y_f, y_r, p, norm_w.reshape(1, d))


def _conv_body(x_ref, pv_ref, nx_ref, w_ref, b_ref, *o_refs, tm, n_ctx, ts, silu):
    row0 = pl.program_id(0) * tm
    x = x_ref[...]
    at_start = jnp.logical_or(row0 == 0, row0 == n_ctx)
    at_end = jnp.logical_or(row0 + tm == n_ctx, row0 + tm == ts)
    pv = jnp.where(at_start, 0.0, pv_ref[...])
    nx = jnp.where(at_end, 0.0, nx_ref[...])
    r = lax.broadcasted_iota(jnp.int32, (tm, 1), 0)
    xm1 = jnp.where(r >= 1, pltpu.roll(x, 1, 0), pv[7:8])
    xm2 = jnp.where(r >= 2, pltpu.roll(x, 2, 0), jnp.where(r == 1, pv[7:8], pv[6:7]))
    xp1 = jnp.where(r <= tm - 2, pltpu.roll(x, tm - 1, 0), nx[0:1])
    w = w_ref[...]
    y = w[0:1] * xm2 + w[1:2] * xm1 + w[2:3] * x + w[3:4] * xp1 + b_ref[...]
    if silu:
        y = y * jax.nn.sigmoid(y)
    for o in o_refs:
        o[...] = y.astype(o.dtype)


def _conv(x, col0, w, b, n_ctx, silu, dtypes):
    ts = x.shape[0]
    cw = w.shape[1]
    tm = _tile(math.gcd(ts, n_ctx), (256, 128, 64, 8))
    tc = _tile(math.gcd(cw, col0) if col0 else cw, (512, 256, 128))
    c0, r8, last8 = col0 // tc, tm // 8, ts // 8 - 1
    out = pl.BlockSpec((tm, tc), lambda i, j: (i, j))
    return pl.pallas_call(
        functools.partial(_conv_body, tm=tm, n_ctx=n_ctx, ts=ts, silu=silu),
        grid=(ts // tm, cw // tc),
        in_specs=[pl.BlockSpec((tm, tc), lambda i, j: (i, j + c0)),
                  pl.BlockSpec((8, tc), lambda i, j: (jnp.maximum(i * r8 - 1, 0), j + c0)),
                  pl.BlockSpec((8, tc), lambda i, j: (jnp.minimum((i + 1) * r8, last8), j + c0)),
                  pl.BlockSpec((4, tc), lambda i, j: (0, j)),
                  pl.BlockSpec((1, tc), lambda i, j: (0, j))],
        out_specs=[out for _ in dtypes],
        out_shape=[jax.ShapeDtypeStruct((ts, cw), dt) for dt in dtypes],
        compiler_params=_params(2), name="conv",
    )(x, x, x, w, b.reshape(1, cw))


def _lru_body(ra_ref, rx_ref, xc_ref, ba_ref, bx_ref, ls_ref, o_ref, h_ref, a_ref, u_ref, *, tb, rev):
    @pl.when(pl.program_id(1) == 0)
    def _():
        h_ref[...] = jnp.zeros_like(h_ref)

    r = jax.nn.sigmoid(ra_ref[...] + ba_ref[...])
    ig = jax.nn.sigmoid(rx_ref[...] + bx_ref[...])
    log_a = RG_C * r * ls_ref[...]
    a = jnp.exp(log_a)
    a_ref[...] = a
    u_ref[...] = jnp.sqrt(-jnp.tanh(log_a) * (a * a + 1.0)) * ig * xc_ref[...]

    def step(i, h):
        t = tb - 1 - i if rev else i
        h = a_ref[pl.ds(t, 1), :] * h + u_ref[pl.ds(t, 1), :]
        o_ref[pl.ds(t, 1), :] = h
        return h

    h_ref[0:1, :] = lax.fori_loop(0, tb, step, h_ref[0:1, :], unroll=8)


def _lru_scan(ra, rx, xc, ba, bx, log_sig, n_ctx, rev):
    t, w = ra.shape
    tb = _tile(math.gcd(t, n_ctx), (256, 128, 64, 8))
    lb = _tile(w, (512, 256, 128))
    nb, nbc = t // tb, n_ctx // tb
    spec = pl.BlockSpec((tb, lb), lambda j, i: (_block_order(i, nbc, nb, rev), j))
    pspec = pl.BlockSpec((1, lb), lambda j, i: (0, j))
    return pl.pallas_call(
        functools.partial(_lru_body, tb=tb, rev=rev),
        grid=(w // lb, nb),
        in_specs=[spec, spec, spec, pspec, pspec, pspec],
        out_specs=spec,
        out_shape=jax.ShapeDtypeStruct((t, w), F32),
        scratch_shapes=[pltpu.VMEM((8, lb), F32), pltpu.VMEM((tb, lb), F32), pltpu.VMEM((tb, lb), F32)],
        compiler_params=_params(2), name="lru_rev" if rev else "lru_fwd",
    )(ra, rx, xc, ba.reshape(1, w), bx.reshape(1, w), log_sig.reshape(1, w))


def _lru_post_body(hf_ref, hr_ref, g_ref, o_ref):
    o_ref[...] = ((hf_ref[...] + hr_ref[...]) * jax.nn.gelu(g_ref[...])).astype(o_ref.dtype)


def _lru_post(h_f, h_r, p):
    t, w = h_f.shape
    tm = _tile(t, (256, 128, 64, 8))
    row = pl.BlockSpec((tm, w), lambda i: (i, 0))
    return pl.pallas_call(
        _lru_post_body,
        grid=(t // tm,),
        in_specs=[row, row, row],
        out_specs=row,
        out_shape=jax.ShapeDtypeStruct((t, w), BF16),
        compiler_params=_params(1), name="lru_post",
    )(h_f, h_r, p)


def _ssd_body(x_ref, b_ref, c_ref, dtc_ref, dac_ref, dtr_ref, dar_ref, o_ref, s_ref, xw_ref, dec_ref, *, gs, r, p, rev):
    c = CHUNK
    gw = r * p

    @pl.when(pl.program_id(1) == 0)
    def _():
        s_ref[...] = jnp.zeros_like(s_ref)

    upto, _, _, row_last = _scan_masks(c, rev)
    tri = jnp.where(upto, 1.0, 0.0).astype(BF16)
    tri_t = jnp.where(_scan_masks(c, not rev)[0], 1.0, 0.0).astype(BF16)
    qs = range(gs)
    xb = x_ref[...]
    bm = [b_ref[:, q * M2_N:(q + 1) * M2_N] for q in qs]
    cm = [c_ref[:, q * M2_N:(q + 1) * M2_N] for q in qs]
    cb = [_bdot(cm[q], bm[q], _NT) for q in qs]
    cs = [_bdot(cm[q], s_ref[q]) for q in qs]
    cum_col = [_cumsum_rows(tri, dac_ref[q]) for q in qs]
    cum_row = [_cumsum_cols(dar_ref[q], tri_t) for q in qs]
    last = [cc[row_last:row_last + 1, :] for cc in cum_col]
    e_in = [jnp.exp(cc) for cc in cum_col]
    wcol = [jnp.exp(last[q] - cum_col[q]) * dtc_ref[q] for q in qs]
    e_last = [jnp.exp(l) for l in last]
    dt_row = [dtr_ref[q] for q in qs]
    qh = [(q, h) for q in qs for h in range(r)]
    sl = {(q, h): slice(q * gw + h * p, q * gw + (h + 1) * p) for q, h in qh}
    m = {(q, h): cb[q] * jnp.exp(jnp.where(upto, cum_col[q][:, h:h + 1] - cum_row[q][h:h + 1, :], -jnp.inf))
         * dt_row[q][h:h + 1, :] for q, h in qh}
    y = {k: _bdot(m[k], xb[:, sl[k]]) for k in qh}
    for q, h in qh:
        k = (q, h)
        o_ref[:, sl[k]] = y[k] + cs[q][:, h * p:(h + 1) * p] * e_in[q][:, h:h + 1]
        xw_ref[:, sl[k]] = (xb[:, sl[k]] * wcol[q][:, h:h + 1]).astype(BF16)
        dec_ref[:, sl[k]] = jnp.broadcast_to(e_last[q][:, h:h + 1], (8, p))
    for q in qs:
        gsl = slice(q * gw, (q + 1) * gw)
        s_ref[q] = s_ref[q] * dec_ref[0:1, gsl] + _bdot(bm[q], xw_ref[:, gsl], _TN)


def _ssd_scan(xbc, di, dt, da, n_ctx, rev):
    t = xbc.shape[0]
    g = M2_G
    r = di // (g * M2_P)
    gw = r * M2_P
    gs = 1
    nch, ncc = t // CHUNK, n_ctx // CHUNK
    b0, c0 = di // (gs * M2_N), (di + g * M2_N) // (gs * M2_N)
    col = lambda z: z.reshape(t, g, r).transpose(1, 0, 2)
    row = lambda z: z.reshape(nch, CHUNK, g, r).transpose(2, 0, 3, 1)
    order = lambda c: _block_order(c, ncc, nch, rev)
    xspec = pl.BlockSpec((CHUNK, gs * gw), lambda gi, c: (order(c), gi))
    bspec = pl.BlockSpec((CHUNK, gs * M2_N), lambda gi, c: (order(c), b0 + gi))
    cspec = pl.BlockSpec((CHUNK, gs * M2_N), lambda gi, c: (order(c), c0 + gi))
    tcol = pl.BlockSpec((gs, CHUNK, r), lambda gi, c: (gi, order(c), 0))
    trow = pl.BlockSpec((gs, None, r, CHUNK), lambda gi, c: (gi, order(c), 0, 0))
    return pl.pallas_call(
        functools.partial(_ssd_body, gs=gs, r=r, p=M2_P, rev=rev),
        grid=(g // gs, nch),
        in_specs=[xspec, bspec, cspec, tcol, tcol, trow, trow],
        out_specs=xspec,
        out_shape=jax.ShapeDtypeStruct((t, di), F32),
        scratch_shapes=[pltpu.VMEM((gs, M2_N, gw), F32), pltpu.VMEM((CHUNK, gs * gw), BF16),
                        pltpu.VMEM((8, gs * gw), F32)],
        compiler_params=_params(2), name="ssd_rev" if rev else "ssd_fwd",
    )(xbc, xbc, xbc, col(dt), col(da), row(dt), row(da))


def _ssd_post_body(yf_ref, yr_ref, x_ref, z_ref, dsk_ref, w_ref, o_ref):
    z = z_ref[...]
    y = (yf_ref[...] + yr_ref[...] + dsk_ref[...] * x_ref[...]) * (z * jax.nn.sigmoid(z))
    o_ref[...] = (y * lax.rsqrt(jnp.mean(y * y, axis=-1, keepdims=True) + NORM_EPS) * w_ref[...]).astype(o_ref.dtype)


def _ssd_post(y_f, y_r, xbc, zx, d_skip, norm_w):
    t, di = y_f.shape
    gw = di // M2_G
    tm = _tile(t, (256, 128, 64, 8))
    blk = pl.BlockSpec((tm, gw), lambda i, g: (i, g))
    par = pl.BlockSpec((1, gw), lambda i, g: (0, g))
    return pl.pallas_call(
        _ssd_post_body,
        grid=(t // tm, M2_G),
        in_specs=[blk, blk, blk, blk, par, par],
        out_specs=blk,
        out_shape=jax.ShapeDtypeStruct((t, di), BF16),
        compiler_params=_params(2), name="ssd_post",
    )(y_f, y_r, xbc, zx, jnp.repeat(d_skip, M2_P).reshape(1, di), norm_w.reshape(1, di))


def _to_scan(x, n_ctx, column_major):
    if not column_major:
        return x
    lat = x[n_ctx:]
    t, d = lat.shape
    lat = lat.reshape(t // GRID_W, GRID_W, d).transpose(1, 0, 2).reshape(t, d)
    return jnp.concatenate([x[:n_ctx], lat], axis=0)


def _to_grid(x, n_ctx, column_major):
    if not column_major:
        return x
    lat = x[n_ctx:]
    t, d = lat.shape
    lat = lat.reshape(GRID_W, t // GRID_W, d).transpose(1, 0, 2).reshape(t, d)
    return jnp.concatenate([x[:n_ctx], lat], axis=0)


def _shifted(x, n_ctx, off):
    ts = x.shape[0]
    t = jnp.arange(ts)[:, None]
    lo = jnp.where(t < n_ctx, 0, n_ctx)
    hi = jnp.where(t < n_ctx, n_ctx, ts)
    ok = (t + off >= lo) & (t + off < hi)
    return jnp.where(ok, jnp.roll(x, -off, axis=0), 0.0)


def _rwkv7_mixer(u, n_ctx, mix, w_rkv, w0, w1, w2, a0, a1, a2, g1, g2, k_k, k_a, r_k, lnx_w, lnx_b):
    ts, d = u.shape
    hn = d // RW_N
    xx = 0.5 * (_shifted(u, n_ctx, -1) + _shifted(u, n_ctx, 1)) - u
    xs = [(u + xx * mix[j]).astype(BF16) for j in range(6)]
    r, k, v = (_linear(xs[j], w_rkv[j]) for j in range(3))
    gate = _linear(jax.nn.sigmoid(_linear(xs[5], g1)).astype(BF16), g2)
    heads = lambda z: z.reshape(ts, hn, RW_N)
    kk = heads(k * k_k)
    kk = (kk * lax.rsqrt(jnp.sum(kk * kk, axis=-1, keepdims=True) + 1e-12)).reshape(ts, d)
    kds, ys = [], []
    for dr in range(2):
        zw = _linear(jnp.tanh(_linear(xs[3], w1[dr])).astype(BF16), w2[dr])
        za = _linear(_linear(xs[4], a1[dr]).astype(BF16), a2[dr])
        kds.append(k * (1.0 + (jax.nn.sigmoid(a0[dr] + za) - 1.0) * k_a))
        ys.append(_rwkv_scan(r, k, v, kk, zw, za, w0[dr], a0[dr], k_a, n_ctx, dr == 1))
    y = heads(ys[0] + ys[1])
    mu = jnp.mean(y, axis=-1, keepdims=True)
    var = jnp.mean(jnp.square(y - mu), axis=-1, keepdims=True)
    yn = ((y - mu) * lax.rsqrt(var + RW_GN_EPS)).reshape(ts, d) * lnx_w + lnx_b
    rh = heads(r)
    bonus = (jnp.sum(rh * heads(kds[0]) * r_k, axis=-1, keepdims=True)
             + jnp.sum(rh * heads(kds[1]) * r_k, axis=-1, keepdims=True))
    return ((yn + (bonus * heads(v)).reshape(ts, d)) * gate).astype(BF16)


def _hgrn2_mixer(u, n_ctx, layer, w_in, lb_logits, norm_w):
    cum = jnp.cumsum(jax.nn.softmax(lb_logits, axis=1), axis=1)
    lb = cum[:, layer] - cum[:, 0]
    d = u.shape[1]
    p = _linear(u, w_in)
    ys = [_gla_scan(p, lb[dr], d, n_ctx, dr == 1) for dr in range(2)]
    return _gla_post(ys[0], ys[1], p, norm_w)


def _rglru_mixer(u, n_ctx, w_in, conv_w, conv_b, wa, ba, wx, bx, lam):
    log_sig = jax.nn.log_sigmoid(lam)
    p = _linear(u, w_in)
    w = p.shape[1] // 2
    xc, xcb = _conv(p, w, conv_w, conv_b, n_ctx, False, (F32, BF16))
    hs = []
    for dr in range(2):
        ra = _block_diag(xcb, wa[dr].astype(BF16))
        rx = _block_diag(xcb, wx[dr].astype(BF16))
        hs.append(_lru_scan(ra, rx, xc, ba[dr], bx[dr], log_sig[dr], n_ctx, dr == 1))
    return _lru_post(hs[0], hs[1], p)


def _mamba2_mixer(u, n_ctx, w_in, conv_w, conv_b, dt_bias, a_log, d_skip, norm_w):
    ts, d = u.shape
    di = 2 * d
    gn = M2_G * M2_N
    hh = di // M2_P
    zx = _linear(u, w_in, n_out=di + di + 2 * gn)
    dt_raw = _linear(u, w_in, col0=di + di + 2 * gn, n_out=2 * hh)
    xbc, = _conv(zx, di, conv_w, conv_b, n_ctx, True, (F32,))
    a_neg = -jnp.exp(a_log)
    dt = jax.nn.softplus(dt_raw.reshape(ts, 2, hh) + dt_bias)
    ys = [_ssd_scan(xbc, di, dt[:, dr], dt[:, dr] * a_neg[dr], n_ctx, dr == 1) for dr in range(2)]
    return _ssd_post(ys[0], ys[1], xbc, zx, d_skip, norm_w)


def _moe(v32, vb, router, w_gu, w_down, res, gate, n_ctx):
    e = router.shape[1]
    logits = _linear(v32, router, precision=lax.Precision.HIGHEST)
    top_val, top_idx = lax.top_k(logits, TOP_K)
    gates = jax.nn.softmax(top_val, axis=-1)
    combine = jnp.sum(jax.nn.one_hot(top_idx, e, dtype=F32) * gates[..., None], axis=1)
    act = _moe_up(vb, w_gu.astype(BF16), combine.T[..., None])
    wd = w_down.astype(BF16).reshape(-1, w_down.shape[-1])
    return _linear_residual(act, wd, res, gate, n_ctx)


def kernel(x, c, ctx, c_ctx, mod_down, mod_up, mod_b, norm_w, final_norm_w, rw_mix, rw_w_rkv, rw_w0, rw_w1, rw_w2, rw_a0, rw_a1, rw_a2, rw_g1, rw_g2, rw_k_k, rw_k_a, rw_r_k, rw_lnx_w, rw_lnx_b, rw_w_o, hg_w_in, hg_lb, hg_norm_w, hg_w_o, rg_w_in, rg_conv_w, rg_conv_b, rg_wa, rg_ba, rg_wx, rg_bx, rg_lam, rg_w_out, m2_w_in, m2_conv_w, m2_conv_b, m2_dt_bias, m2_a_log, m2_d, m2_norm_w, m2_w_out, ffn_w_gu, ffn_w_down, moe_router, moe_w_gu, moe_w_down):
    assert x.shape[0] == 1 and c.shape[0] == 1
    depth = mod_down.shape[0]
    n_ctx, d = ctx.shape[1], ctx.shape[2]
    xh = jnp.concatenate([ctx[0], x[0]], axis=0)
    cvec = jnp.zeros((8, d), F32).at[0].set(c_ctx).at[1].set(c[0])
    cvec = jax.nn.silu(cvec).astype(BF16)
    for i in range(depth):
        kind = i % 4
        col_major = i % 2 == 1
        low = _linear(cvec, mod_down[i]).astype(BF16)
        mod = (_linear(low, mod_up[i])[:2] + mod_b[i]).reshape(2, 6, d)

        want = (F32,) if kind == 0 else (BF16,)
        u = _to_scan(_norm_mod(xh, norm_w[i, 0], mod, 0, 1, n_ctx, want)[0], n_ctx, col_major)
        if kind == 0:
            o = _rwkv7_mixer(u, n_ctx, rw_mix, rw_w_rkv, rw_w0, rw_w1, rw_w2, rw_a0, rw_a1, rw_a2,
                             rw_g1, rw_g2, rw_k_k, rw_k_a, rw_r_k, rw_lnx_w, rw_lnx_b)
            w_out = rw_w_o
        elif kind == 1:
            o = _hgrn2_mixer(u, n_ctx, i, hg_w_in, hg_lb, hg_norm_w)
            w_out = hg_w_o
        elif kind == 2:
            o = _rglru_mixer(u, n_ctx, rg_w_in, rg_conv_w, rg_conv_b, rg_wa, rg_ba, rg_wx, rg_bx, rg_lam)
            w_out = rg_w_out
        else:
            o = _mamba2_mixer(u, n_ctx, m2_w_in, m2_conv_w, m2_conv_b, m2_dt_bias, m2_a_log, m2_d, m2_norm_w)
            w_out = m2_w_out
        xh = _linear_residual(_to_grid(o, n_ctx, col_major), _wt(w_out), xh, mod[:, 2], n_ctx)

        fi = i // 2
        if i % 2 == 0:
            vb, = _norm_mod(xh, norm_w[i, 1], mod, 4, 3, n_ctx, (BF16,))
            act = _swiglu_up(vb, ffn_w_gu[fi].astype(BF16))
            xh = _linear_residual(act, ffn_w_down[fi].astype(BF16), xh, mod[:, 5], n_ctx)
        else:
            v32, vb = _norm_mod(xh, norm_w[i, 1], mod, 4, 3, n_ctx, (F32, BF16))
            xh = _moe(v32, vb, moe_router[fi], moe_w_gu[fi], moe_w_down[fi], xh, mod[:, 5], n_ctx)
    out = _rmsnorm(xh[n_ctx:], final_norm_w, F32)
    return out[None]
```

```python
import functools
import math

import jax
import jax.numpy as jnp
from jax import lax
from jax.experimental import pallas as pl
from jax.experimental.pallas import tpu as pltpu

F32 = jnp.float32
BF16 = jnp.bfloat16

NORM_EPS = 1e-6
GRID_W = 64
CHUNK = 64
RW_N = 64
RW_GN_EPS = 64e-5
HG_K = 128
RG_H = 16
RG_C = 8.0
M2_P = 64
M2_G = 8
M2_N = 128
TOP_K = 2

_VMEM_LIMIT = 56 * 1024 * 1024

_NN = (((1,), (0,)), ((), ()))
_NT = (((1,), (1,)), ((), ()))
_TN = (((0,), (0,)), ((), ()))


def _params(n_axes):
    return pltpu.CompilerParams(dimension_semantics=("arbitrary",) * n_axes,
                                vmem_limit_bytes=_VMEM_LIMIT)


def _tile(n, prefs):
    for p in prefs:
        if n % p == 0:
            return p
    return n


def _bdot(a, b, dims=_NN):
    return lax.dot_general(a.astype(BF16), b.astype(BF16), dims, preferred_element_type=F32)


def _split3(x):
    hi = x.astype(BF16)
    r1 = x - hi.astype(F32)
    mid = r1.astype(BF16)
    lo = (r1 - mid.astype(F32)).astype(BF16)
    return hi, mid, lo


def _cumsum_rows(tri, x):
    hi, mid, lo = _split3(x)
    d = lambda p: jnp.dot(tri, p, preferred_element_type=F32)
    return d(hi) + d(mid) + d(lo)


def _cumsum_cols(x, triu):
    hi, mid, lo = _split3(x)
    d = lambda p: jnp.dot(p, triu, preferred_element_type=F32)
    return d(hi) + d(mid) + d(lo)


def _scan_masks(c, rev):
    ti = lax.broadcasted_iota(jnp.int32, (c, c), 0)
    si = lax.broadcasted_iota(jnp.int32, (c, c), 1)
    upto = (ti <= si) if rev else (ti >= si)
    before = (ti < si) if rev else (ti > si)
    eye = jnp.where(ti == si, 1.0, 0.0)
    return upto, before, eye, (0 if rev else c - 1)


def _block_order(i, n_first, n_total, rev):
    if not rev:
        return i
    return jnp.where(i < n_first, n_first - 1 - i, n_total + n_first - 1 - i)


def _norm_mod_body(x_ref, w_ref, mod_ref, *o_refs, shift_i, scale_i, n_ctx, tm):
    x = x_ref[...]
    y = x * lax.rsqrt(jnp.mean(x * x, axis=-1, keepdims=True) + NORM_EPS) * w_ref[...]
    row = pl.program_id(0) * tm + lax.broadcasted_iota(jnp.int32, (tm, 1), 0)
    is_ctx = row < n_ctx
    shift = jnp.where(is_ctx, mod_ref[0, shift_i:shift_i + 1, :], mod_ref[1, shift_i:shift_i + 1, :])
    scale = jnp.where(is_ctx, mod_ref[0, scale_i:scale_i + 1, :], mod_ref[1, scale_i:scale_i + 1, :])
    y = y * (1.0 + scale) + shift
    for o in o_refs:
        o[...] = y.astype(o.dtype)


def _norm_mod(xh, w, mod, shift_i, scale_i, n_ctx, dtypes):
    m, d = xh.shape
    tm = _tile(m, (256, 128, 64, 8))
    out = pl.pallas_call(
        functools.partial(_norm_mod_body, shift_i=shift_i, scale_i=scale_i, n_ctx=n_ctx, tm=tm),
        grid=(m // tm,),
        in_specs=[pl.BlockSpec((tm, d), lambda i: (i, 0)),
                  pl.BlockSpec((1, d), lambda i: (0, 0)),
                  pl.BlockSpec((2, 6, d), lambda i: (0, 0, 0))],
        out_specs=[pl.BlockSpec((tm, d), lambda i: (i, 0)) for _ in dtypes],
        out_shape=[jax.ShapeDtypeStruct((m, d), dt) for dt in dtypes],
        compiler_params=_params(1), name="norm_mod",
    )(xh, w.reshape(1, d), mod)
    return out


def _rmsnorm_body(x_ref, w_ref, o_ref, *, groups):
    x = x_ref[...]
    gw = x.shape[-1] // groups
    for g in range(groups):
        xs = x[:, g * gw:(g + 1) * gw]
        o_ref[:, g * gw:(g + 1) * gw] = (
            xs * lax.rsqrt(jnp.mean(xs * xs, axis=-1, keepdims=True) + NORM_EPS)
            * w_ref[:, g * gw:(g + 1) * gw]).astype(o_ref.dtype)


def _rmsnorm(x, w, out_dtype, groups=1):
    m, d = x.shape
    tm = _tile(m, (256, 128, 64, 8))
    return pl.pallas_call(
        functools.partial(_rmsnorm_body, groups=groups),
        grid=(m // tm,),
        in_specs=[pl.BlockSpec((tm, d), lambda i: (i, 0)), pl.BlockSpec((1, d), lambda i: (0, 0))],
        out_specs=pl.BlockSpec((tm, d), lambda i: (i, 0)),
        out_shape=jax.ShapeDtypeStruct((m, d), out_dtype),
        compiler_params=_params(1), name="rmsnorm",
    )(x, w.reshape(1, d))


def _mm_body(*refs, nw, nex, nk, epi, precision):
    x_ref = refs[0]
    w_refs = refs[1:1 + nw]
    ex_refs = refs[1 + nw:1 + nw + nex]
    o_ref = refs[1 + nw + nex]
    acc_refs = refs[2 + nw + nex:]
    x = x_ref[...]
    prods = [jnp.dot(x, w[...] if precision is not None else w[...].astype(x.dtype),
                     preferred_element_type=F32, precision=precision) for w in w_refs]

    def finish(accs):
        o_ref[...] = epi(accs, [e[...] for e in ex_refs], pl.program_id(1)).astype(o_ref.dtype)

    if nk == 1:
        finish(prods)
    else:
        k = pl.program_id(2)

        @pl.when(k == 0)
        def _():
            for a, p in zip(acc_refs, prods):
                a[...] = p

        @pl.when(k > 0)
        def _():
            for a, p in zip(acc_refs, prods):
                a[...] += p

        @pl.when(k == nk - 1)
        def _():
            finish([a[...] for a in acc_refs])


def _matmul(x, ws, w_specs, *, n_out, epi, tm, tn, tk=None, extras=(), extra_specs=(),
            out_dtype=F32, x_map=None, k_total=None, precision=None, name="mm"):
    m = x.shape[0]
    k_total = k_total or x.shape[1]
    tk = tk or k_total
    nk = k_total // tk
    x_map = x_map or (lambda j, i, k: (i, k))
    return pl.pallas_call(
        functools.partial(_mm_body, nw=len(ws), nex=len(extras), nk=nk, epi=epi, precision=precision),
        grid=(n_out // tn, m // tm, nk),
        in_specs=[pl.BlockSpec((tm, tk), x_map)] + list(w_specs) + list(extra_specs),
        out_specs=pl.BlockSpec((tm, tn), lambda j, i, k: (i, j)),
        out_shape=jax.ShapeDtypeStruct((m, n_out), out_dtype),
        scratch_shapes=[pltpu.VMEM((tm, tn), F32) for _ in ws] if nk > 1 else [],
        compiler_params=_params(3), name=name,
    )(x, *ws, *extras)


def _wt(w):
    return w if w.shape[-2] <= 4096 else w.astype(BF16)


def _epi_plain(accs, extras, i):
    return accs[0]


def _row_tile(m):
    return _tile(m, (768, 512, 640, 384, 256, 128, 64, 8))


def _k_tile(k):
    return k if k <= 4096 else _tile(k, (4096, 2048, 1024))


def _linear(x, w, out_dtype=F32, col0=0, n_out=None, precision=None):
    m, k = x.shape
    n_out = n_out or w.shape[1]
    tn = _tile(math.gcd(n_out, col0) if col0 else n_out, (512, 256, 128))
    tn = min(tn, n_out)
    c0 = col0 // tn
    tk = _k_tile(k)
    return _matmul(x, [w], [pl.BlockSpec((tk, tn), lambda j, i, kk: (kk, j + c0))],
                   n_out=n_out, epi=_epi_plain, tm=_row_tile(m), tn=tn, tk=tk,
                   out_dtype=out_dtype, precision=precision)


def _linear_residual(x, w, res, gate, n_ctx):
    m, k = x.shape
    n = w.shape[1]
    tm, tn, tk = _row_tile(m), _tile(n, (512, 256, 128)), _k_tile(k)

    def epi(accs, extras, i):
        r, g = extras
        row = i * tm + lax.broadcasted_iota(jnp.int32, (tm, 1), 0)
        return r + jnp.where(row < n_ctx, g[0:1, :], g[1:2, :]) * accs[0]

    return _matmul(x, [w], [pl.BlockSpec((tk, tn), lambda j, i, kk: (kk, j))],
                   n_out=n, epi=epi, tm=tm, tn=tn, tk=tk, extras=(res, gate),
                   extra_specs=(pl.BlockSpec((tm, tn), lambda j, i, kk: (i, j)),
                                pl.BlockSpec((2, tn), lambda j, i, kk: (0, j))))


def _epi_swiglu(accs, extras, i):
    g, u = accs
    return g * jax.nn.sigmoid(g) * u


def _swiglu_up(x, w_gu):
    m, k = x.shape
    f = w_gu.shape[1] // 2
    tn = _tile(f, (512, 256, 128))
    nj = f // tn
    return _matmul(x, [w_gu, w_gu],
                   [pl.BlockSpec((k, tn), lambda j, i, kk: (0, j)),
                    pl.BlockSpec((k, tn), lambda j, i, kk: (0, j + nj))],
                   n_out=f, epi=_epi_swiglu, tm=_row_tile(m), tn=tn, out_dtype=BF16)


def _block_diag(x, w):
    m = x.shape[0]
    h, bw, _ = w.shape
    return _matmul(x, [w], [pl.BlockSpec((None, bw, bw), lambda j, i, kk: (j, 0, 0))],
                   n_out=h * bw, epi=_epi_plain, tm=_row_tile(m), tn=bw, tk=bw,
                   x_map=lambda j, i, kk: (i, j), k_total=bw)


def _rwkv_body(r_ref, k_ref, v_ref, kk_ref, zw_ref, za_ref, w0_ref, a0_ref, ka_ref, o_ref, s_ref, *, heads, n, rev):
    c = CHUNK

    @pl.when(pl.program_id(1) == 0)
    def _():
        s_ref[...] = jnp.zeros_like(s_ref)

    upto, before, eye, row_last = _scan_masks(c, rev)
    tri = jnp.where(upto, 1.0, 0.0).astype(BF16)
    lw = -math.exp(-0.5) * jax.nn.sigmoid(w0_ref[...] + zw_ref[...])
    a_gate = jax.nn.sigmoid(a0_ref[...] + za_ref[...])
    kk = kk_ref[...]
    kd = k_ref[...] * (1.0 + (a_gate - 1.0) * ka_ref[...])
    b = kk * a_gate
    cum = _cumsum_rows(tri, lw)
    last = cum[row_last:row_last + 1, :]
    g_inv = jnp.exp(-cum)
    tail = jnp.exp(last - cum)
    dec = jnp.exp(last)
    at = (-kk * jnp.exp(cum - lw)).astype(BF16)
    rt = (r_ref[...] * jnp.exp(cum)).astype(BF16)
    bt, kt = (b * g_inv).astype(BF16), (kd * g_inv).astype(BF16)
    bg, kg = (b * tail).astype(BF16), (kd * tail).astype(BF16)
    vb = v_ref[...].astype(BF16)
    hs = range(heads)
    sl = [slice(h * n, (h + 1) * n) for h in hs]
    lhs = [jnp.concatenate([at[:, s], rt[:, s]], axis=0) for s in sl]
    rhs = [jnp.concatenate([bt[:, s], kt[:, s]], axis=0) for s in sl]
    s0 = [s_ref[h] for h in hs]
    p = [_bdot(lhs[h], rhs[h], _NT) for h in hs]
    lm = [_bdot(lhs[h], s0[h], _NT) for h in hs]
    x = [jnp.where(before, p[h][:c, :c], 0.0) for h in hs]
    t = [eye + xh for xh in x]
    for _ in range(int(math.log2(c)) - 1):
        x = [_bdot(xh, xh) for xh in x]
        t = [th + _bdot(th, xh) for th, xh in zip(t, x)]
    w0 = [lm[h][:c] + _bdot(jnp.where(before, p[h][:c, c:], 0.0), vb[:, sl[h]]) for h in hs]
    u = [_bdot(t[h], w0[h]) for h in hs]
    uv = [jnp.concatenate([u[h].astype(BF16), vb[:, sl[h]]], axis=0) for h in hs]
    for h in hs:
        r_bk = jnp.concatenate([jnp.where(upto, p[h][c:, :c], 0.0), jnp.where(upto, p[h][c:, c:], 0.0)], axis=1)
        o_ref[:, sl[h]] = lm[h][c:] + _bdot(r_bk, uv[h])
    for h in hs:
        bk = jnp.concatenate([bg[:, sl[h]], kg[:, sl[h]]], axis=0)
        s_ref[h] = s0[h] * dec[:, sl[h]] + _bdot(uv[h], bk, _TN)


def _rwkv_scan(r, k, v, kk, zw, za, w0, a0, k_a, n_ctx, rev):
    t, d = r.shape
    heads = 16 if d % (16 * RW_N) == 0 else d // RW_N
    w = heads * RW_N
    nch, ncc = t // CHUNK, n_ctx // CHUNK
    spec = pl.BlockSpec((CHUNK, w), lambda g, c: (_block_order(c, ncc, nch, rev), g))
    pspec = pl.BlockSpec((1, w), lambda g, c: (0, g))
    return pl.pallas_call(
        functools.partial(_rwkv_body, heads=heads, n=RW_N, rev=rev),
        grid=(d // w, nch),
        in_specs=[spec] * 6 + [pspec] * 3,
        out_specs=spec,
        out_shape=jax.ShapeDtypeStruct((t, d), F32),
        scratch_shapes=[pltpu.VMEM((heads, RW_N, RW_N), F32)],
        compiler_params=_params(2), name="rwkv_rev" if rev else "rwkv_fwd",
    )(r, k, v, kk, zw, za, w0.reshape(1, d), a0.reshape(1, d), k_a.reshape(1, d))


def _gla_body(q_ref, zf_ref, v_ref, lb_ref, o_ref, s_ref, *, heads, kdim, sub, rev):
    c = CHUNK

    @pl.when(pl.program_id(1) == 0)
    def _():
        s_ref[...] = jnp.zeros_like(s_ref)

    upto, _, _, row_last = _scan_masks(c, rev)
    tri = jnp.where(upto, 1.0, 0.0).astype(BF16)
    lb, zf = lb_ref[...], zf_ref[...]
    cum = _cumsum_rows(tri, jnp.log(lb + (1.0 - lb) * jax.nn.sigmoid(zf)))
    last = cum[row_last:row_last + 1, :]
    k = (1.0 - lb) * jax.nn.sigmoid(-zf)
    q = q_ref[...]
    q = q * jax.nn.sigmoid(q)
    vb = v_ref[...].astype(BF16)
    qe = (q * jnp.exp(cum)).astype(BF16)
    ktail = (k * jnp.exp(last - cum)).astype(BF16)
    dec = jnp.exp(last)
    hs = range(heads)
    sl = [slice(h * kdim, (h + 1) * kdim) for h in hs]
    s0 = [s_ref[h] for h in hs]
    inter = [_bdot(qe[:, sl[h]], s0[h], _NT) for h in hs]
    nb = c // sub
    for i in range(nb):
        lo, hi = i * sub, (i + 1) * sub
        if rev:
            ks = slice(lo, c)
            piv = cum[hi:hi + 1, :] if i < nb - 1 else jnp.zeros_like(last)
        else:
            ks = slice(0, hi)
            piv = cum[lo - 1:lo, :] if i > 0 else jnp.zeros_like(last)
        nk = ks.stop - ks.start
        qi = (q[lo:hi] * jnp.exp(cum[lo:hi] - piv)).astype(BF16)
        ki = (k[ks] * jnp.exp(piv - cum[ks])).astype(BF16)
        tt = lo + lax.broadcasted_iota(jnp.int32, (sub, nk), 0)
        ss = ks.start + lax.broadcasted_iota(jnp.int32, (sub, nk), 1)
        mask = (tt <= ss) if rev else (tt >= ss)
        att = [jnp.where(mask, _bdot(qi[:, sl[h]], ki[:, sl[h]], _NT), 0.0) for h in hs]
        for h in hs:
            o_ref[lo:hi, sl[h]] = inter[h][lo:hi] + _bdot(att[h], vb[ks, sl[h]])
    for h in hs:
        s_ref[h] = s0[h] * dec[:, sl[h]] + _bdot(vb[:, sl[h]], ktail[:, sl[h]], _TN)


def _gla_scan(p, lb, d, n_ctx, rev):
    t = p.shape[0]
    heads = 8 if d % (8 * HG_K) == 0 else d // HG_K
    w = heads * HG_K
    nd = d // w
    nch, ncc = t // CHUNK, n_ctx // CHUNK
    zf_col = (4 if rev else 3) * nd

    def spec(col):
        return pl.BlockSpec((CHUNK, w), lambda gi, c: (_block_order(c, ncc, nch, rev), col + gi))

    return pl.pallas_call(
        functools.partial(_gla_body, heads=heads, kdim=HG_K, sub=16, rev=rev),
        grid=(nd, nch),
        in_specs=[spec(0), spec(zf_col), spec(nd), pl.BlockSpec((1, w), lambda gi, c: (0, gi))],
        out_specs=spec(0),
        out_shape=jax.ShapeDtypeStruct((t, d), F32),
        scratch_shapes=[pltpu.VMEM((heads, HG_K, HG_K), F32)],
        compiler_params=_params(2), name="gla_rev" if rev else "gla_fwd",
    )(p, p, p, lb.reshape(1, d))


def _gla_post_body(yf_ref, yr_ref, og_ref, w_ref, o_ref):
    y = yf_ref[...] + yr_ref[...]
    og = og_ref[...]
    y = y * lax.rsqrt(jnp.mean(y * y, axis=-1, keepdims=True) + NORM_EPS) * w_ref[...]
    o_ref[...] = (y * (og * jax.nn.sigmoid(og))).astype(o_ref.dtype)


def _gla_post(y_f, y_r, p, norm_w):
    t, d = y_f.shape
    tm = _tile(t, (256, 128, 64, 8))
    row = pl.BlockSpec((tm, d), lambda i: (i, 0))
    return pl.pallas_call(
        _gla_post_body,
        grid=(t // tm,),
        in_specs=[row, row, pl.BlockSpec((tm, d), lambda i: (i, 2)), pl.BlockSpec((1, d), lambda i: (0, 0))],
        out_specs=row,
        out_shape=jax.ShapeDtypeStruct((t, d), BF16),
        compiler_params=_params(1), name="gla_post",
    )(y_f, y_r, p, norm_w.reshape(1, d))


def _conv_body(x_ref, pv_ref, nx_ref, w_ref, b_ref, *o_refs, tm, n_ctx, ts, silu):
    row0 = pl.program_id(0) * tm
    x = x_ref[...]
    at_start = jnp.logical_or(row0 == 0, row0 == n_ctx)
    at_end = jnp.logical_or(row0 + tm == n_ctx, row0 + tm == ts)
    pv = jnp.where(at_start, 0.0, pv_ref[...])
    nx = jnp.where(at_end, 0.0, nx_ref[...])
    r = lax.broadcasted_iota(jnp.int32, (tm, 1), 0)
    xm1 = jnp.where(r >= 1, pltpu.roll(x, 1, 0), pv[7:8])
    xm2 = jnp.where(r >= 2, pltpu.roll(x, 2, 0), jnp.where(r == 1, pv[7:8], pv[6:7]))
    xp1 = jnp.where(r <= tm - 2, pltpu.roll(x, tm - 1, 0), nx[0:1])
    w = w_ref[...]
    y = w[0:1] * xm2 + w[1:2] * xm1 + w[2:3] * x + w[3:4] * xp1 + b_ref[...]
    if silu:
        y = y * jax.nn.sigmoid(y)
    for o in o_refs:
        o[...] = y.astype(o.dtype)


def _conv(x, col0, w, b, n_ctx, silu, dtypes):
    ts = x.shape[0]
    cw = w.shape[1]
    tm = _tile(math.gcd(ts, n_ctx), (256, 128, 64, 8))
    tc = _tile(math.gcd(cw, col0) if col0 else cw, (512, 256, 128))
    c0, r8, last8 = col0 // tc, tm // 8, ts // 8 - 1
    out = pl.BlockSpec((tm, tc), lambda i, j: (i, j))
    return pl.pallas_call(
        functools.partial(_conv_body, tm=tm, n_ctx=n_ctx, ts=ts, silu=silu),
        grid=(ts // tm, cw // tc),
        in_specs=[pl.BlockSpec((tm, tc), lambda i, j: (i, j + c0)),
                  pl.BlockSpec((8, tc), lambda i, j: (jnp.maximum(i * r8 - 1, 0), j + c0)),
                  pl.BlockSpec((8, tc), lambda i, j: (jnp.minimum((i + 1) * r8, last8), j + c0)),
                  pl.BlockSpec((4, tc), lambda i, j: (0, j)),
                  pl.BlockSpec((1, tc), lambda i, j: (0, j))],
        out_specs=[out for _ in dtypes],
        out_shape=[jax.ShapeDtypeStruct((ts, cw), dt) for dt in dtypes],
        compiler_params=_params(2), name="conv",
    )(x, x, x, w, b.reshape(1, cw))


def _lru_body(ra_ref, rx_ref, xc_ref, ba_ref, bx_ref, ls_ref, o_ref, h_ref, a_ref, u_ref, *, tb, rev):
    @pl.when(pl.program_id(1) == 0)
    def _():
        h_ref[...] = jnp.zeros_like(h_ref)

    r = jax.nn.sigmoid(ra_ref[...] + ba_ref[...])
    ig = jax.nn.sigmoid(rx_ref[...] + bx_ref[...])
    log_a = RG_C * r * ls_ref[...]
    a = jnp.exp(log_a)
    a_ref[...] = a
    u_ref[...] = jnp.sqrt(-jnp.tanh(log_a) * (a * a + 1.0)) * ig * xc_ref[...]

    def step(i, h):
        t = tb - 1 - i if rev else i
        h = a_ref[pl.ds(t, 1), :] * h + u_ref[pl.ds(t, 1), :]
        o_ref[pl.ds(t, 1), :] = h
        return h

    h_ref[0:1, :] = lax.fori_loop(0, tb, step, h_ref[0:1, :], unroll=8)


def _lru_scan(ra, rx, xc, ba, bx, log_sig, n_ctx, rev):
    t, w = ra.shape
    tb = _tile(math.gcd(t, n_ctx), (256, 128, 64, 8))
    lb = _tile(w, (512, 256, 128))
    nb, nbc = t // tb, n_ctx // tb
    spec = pl.BlockSpec((tb, lb), lambda j, i: (_block_order(i, nbc, nb, rev), j))
    pspec = pl.BlockSpec((1, lb), lambda j, i: (0, j))
    return pl.pallas_call(
        functools.partial(_lru_body, tb=tb, rev=rev),
        grid=(w // lb, nb),
        in_specs=[spec, spec, spec, pspec, pspec, pspec],
        out_specs=spec,
        out_shape=jax.ShapeDtypeStruct((t, w), F32),
        scratch_shapes=[pltpu.VMEM((8, lb), F32), pltpu.VMEM((tb, lb), F32), pltpu.VMEM((tb, lb), F32)],
        compiler_params=_params(2), name="lru_rev" if rev else "lru_fwd",
    )(ra, rx, xc, ba.reshape(1, w), bx.reshape(1, w), log_sig.reshape(1, w))


def _lru_post_body(hf_ref, hr_ref, g_ref, o_ref):
    o_ref[...] = ((hf_ref[...] + hr_ref[...]) * jax.nn.gelu(g_ref[...])).astype(o_ref.dtype)


def _lru_post(h_f, h_r, p):
    t, w = h_f.shape
    tm = _tile(t, (256, 128, 64, 8))
    row = pl.BlockSpec((tm, w), lambda i: (i, 0))
    return pl.pallas_call(
        _lru_post_body,
        grid=(t // tm,),
        in_specs=[row, row, row],
        out_specs=row,
        out_shape=jax.ShapeDtypeStruct((t, w), BF16),
        compiler_params=_params(1), name="lru_post",
    )(h_f, h_r, p)


def _ssd_body(x_ref, b_ref, c_ref, dtc_ref, dac_ref, dtr_ref, dar_ref, o_ref, s_ref, xw_ref, dec_ref, *, gs, r, p, rev):
    c = CHUNK
    gw = r * p

    @pl.when(pl.program_id(1) == 0)
    def _():
        s_ref[...] = jnp.zeros_like(s_ref)

    upto, _, _, row_last = _scan_masks(c, rev)
    tri = jnp.where(upto, 1.0, 0.0).astype(BF16)
    tri_t = jnp.where(_scan_masks(c, not rev)[0], 1.0, 0.0).astype(BF16)
    qs = range(gs)
    xb = x_ref[...]
    bm = [b_ref[:, q * M2_N:(q + 1) * M2_N] for q in qs]
    cm = [c_ref[:, q * M2_N:(q + 1) * M2_N] for q in qs]
    cb = [_bdot(cm[q], bm[q], _NT) for q in qs]
    cs = [_bdot(cm[q], s_ref[q]) for q in qs]
    cum_col = [_cumsum_rows(tri, dac_ref[q]) for q in qs]
    cum_row = [_cumsum_cols(dar_ref[q], tri_t) for q in qs]
    last = [cc[row_last:row_last + 1, :] for cc in cum_col]
    e_in = [jnp.exp(cc) for cc in cum_col]
    wcol = [jnp.exp(last[q] - cum_col[q]) * dtc_ref[q] for q in qs]
    e_last = [jnp.exp(l) for l in last]
    dt_row = [dtr_ref[q] for q in qs]
    qh = [(q, h) for q in qs for h in range(r)]
    sl = {(q, h): slice(q * gw + h * p, q * gw + (h + 1) * p) for q, h in qh}
    m = {(q, h): cb[q] * jnp.exp(jnp.where(upto, cum_col[q][:, h:h + 1] - cum_row[q][h:h + 1, :], -jnp.inf))
         * dt_row[q][h:h + 1, :] for q, h in qh}
    y = {k: _bdot(m[k], xb[:, sl[k]]) for k in qh}
    for q, h in qh:
        k = (q, h)
        o_ref[:, sl[k]] = y[k] + cs[q][:, h * p:(h + 1) * p] * e_in[q][:, h:h + 1]
        xw_ref[:, sl[k]] = (xb[:, sl[k]] * wcol[q][:, h:h + 1]).astype(BF16)
        dec_ref[:, sl[k]] = jnp.broadcast_to(e_last[q][:, h:h + 1], (8, p))
    for q in qs:
        gsl = slice(q * gw, (q + 1) * gw)
        s_ref[q] = s_ref[q] * dec_ref[0:1, gsl] + _bdot(bm[q], xw_ref[:, gsl], _TN)


def _ssd_scan(xbc, di, dt, da, n_ctx, rev):
    t = xbc.shape[0]
    g = M2_G
    r = di // (g * M2_P)
    gw = r * M2_P
    gs = 1
    nch, ncc = t // CHUNK, n_ctx // CHUNK
    b0, c0 = di // (gs * M2_N), (di + g * M2_N) // (gs * M2_N)
    col = lambda z: z.reshape(t, g, r).transpose(1, 0, 2)
    row = lambda z: z.reshape(nch, CHUNK, g, r).transpose(2, 0, 3, 1)
    order = lambda c: _block_order(c, ncc, nch, rev)
    xspec = pl.BlockSpec((CHUNK, gs * gw), lambda gi, c: (order(c), gi))
    bspec = pl.BlockSpec((CHUNK, gs * M2_N), lambda gi, c: (order(c), b0 + gi))
    cspec = pl.BlockSpec((CHUNK, gs * M2_N), lambda gi, c: (order(c), c0 + gi))
    tcol = pl.BlockSpec((gs, CHUNK, r), lambda gi, c: (gi, order(c), 0))
    trow = pl.BlockSpec((gs, None, r, CHUNK), lambda gi, c: (gi, order(c), 0, 0))
    return pl.pallas_call(
        functools.partial(_ssd_body, gs=gs, r=r, p=M2_P, rev=rev),
        grid=(g // gs, nch),
        in_specs=[xspec, bspec, cspec, tcol, tcol, trow, trow],
        out_specs=xspec,
        out_shape=jax.ShapeDtypeStruct((t, di), F32),
        scratch_shapes=[pltpu.VMEM((gs, M2_N, gw), F32), pltpu.VMEM((CHUNK, gs * gw), BF16),
                        pltpu.VMEM((8, gs * gw), F32)],
        compiler_params=_params(2), name="ssd_rev" if rev else "ssd_fwd",
    )(xbc, xbc, xbc, col(dt), col(da), row(dt), row(da))


def _ssd_post_body(yf_ref, yr_ref, x_ref, z_ref, dsk_ref, w_ref, o_ref):
    z = z_ref[...]
    y = (yf_ref[...] + yr_ref[...] + dsk_ref[...] * x_ref[...]) * (z * jax.nn.sigmoid(z))
    o_ref[...] = (y * lax.rsqrt(jnp.mean(y * y, axis=-1, keepdims=True) + NORM_EPS) * w_ref[...]).astype(o_ref.dtype)


def _ssd_post(y_f, y_r, xbc, zx, d_skip, norm_w):
    t, di = y_f.shape
    gw = di // M2_G
    tm = _tile(t, (256, 128, 64, 8))
    blk = pl.BlockSpec((tm, gw), lambda i, g: (i, g))
    par = pl.BlockSpec((1, gw), lambda i, g: (0, g))
    return pl.pallas_call(
        _ssd_post_body,
        grid=(t // tm, M2_G),
        in_specs=[blk, blk, blk, blk, par, par],
        out_specs=blk,
        out_shape=jax.ShapeDtypeStruct((t, di), BF16),
        compiler_params=_params(2), name="ssd_post",
    )(y_f, y_r, xbc, zx, jnp.repeat(d_skip, M2_P).reshape(1, di), norm_w.reshape(1, di))


def _to_scan(x, n_ctx, column_major):
    if not column_major:
        return x
    lat = x[n_ctx:]
    t, d = lat.shape
    lat = lat.reshape(t // GRID_W, GRID_W, d).transpose(1, 0, 2).reshape(t, d)
    return jnp.concatenate([x[:n_ctx], lat], axis=0)


def _to_grid(x, n_ctx, column_major):
    if not column_major:
        return x
    lat = x[n_ctx:]
    t, d = lat.shape
    lat = lat.reshape(GRID_W, t // GRID_W, d).transpose(1, 0, 2).reshape(t, d)
    return jnp.concatenate([x[:n_ctx], lat], axis=0)


def _shifted(x, n_ctx, off):
    ts = x.shape[0]
    t = jnp.arange(ts)[:, None]
    lo = jnp.where(t < n_ctx, 0, n_ctx)
    hi = jnp.where(t < n_ctx, n_ctx, ts)
    ok = (t + off >= lo) & (t + off < hi)
    return jnp.where(ok, jnp.roll(x, -off, axis=0), 0.0)


def _rwkv7_mixer(u, n_ctx, mix, w_rkv, w0, w1, w2, a0, a1, a2, g1, g2, k_k, k_a, r_k, lnx_w, lnx_b):
    ts, d = u.shape
    hn = d // RW_N
    xx = 0.5 * (_shifted(u, n_ctx, -1) + _shifted(u, n_ctx, 1)) - u
    xs = [(u + xx * mix[j]).astype(BF16) for j in range(6)]
    r, k, v = (_linear(xs[j], w_rkv[j]) for j in range(3))
    gate = _linear(jax.nn.sigmoid(_linear(xs[5], g1)).astype(BF16), g2)
    heads = lambda z: z.reshape(ts, hn, RW_N)
    kk = heads(k * k_k)
    kk = (kk * lax.rsqrt(jnp.sum(kk * kk, axis=-1, keepdims=True) + 1e-12)).reshape(ts, d)
    kds, ys = [], []
    for dr in range(2):
        zw = _linear(jnp.tanh(_linear(xs[3], w1[dr])).astype(BF16), w2[dr])
        za = _linear(_linear(xs[4], a1[dr]).astype(BF16), a2[dr])
        kds.append(k * (1.0 + (jax.nn.sigmoid(a0[dr] + za) - 1.0) * k_a))
        ys.append(_rwkv_scan(r, k, v, kk, zw, za, w0[dr], a0[dr], k_a, n_ctx, dr == 1))
    y = heads(ys[0] + ys[1])
    mu = jnp.mean(y, axis=-1, keepdims=True)
    var = jnp.mean(jnp.square(y - mu), axis=-1, keepdims=True)
    yn = ((y - mu) * lax.rsqrt(var + RW_GN_EPS)).reshape(ts, d) * lnx_w + lnx_b
    rh = heads(r)
    bonus = (jnp.sum(rh * heads(kds[0]) * r_k, axis=-1, keepdims=True)
             + jnp.sum(rh * heads(kds[1]) * r_k, axis=-1, keepdims=True))
    return ((yn + (bonus * heads(v)).reshape(ts, d)) * gate).astype(BF16)


def _hgrn2_mixer(u, n_ctx, layer, w_in, lb_logits, norm_w):
    cum = jnp.cumsum(jax.nn.softmax(lb_logits, axis=1), axis=1)
    lb = cum[:, layer] - cum[:, 0]
    d = u.shape[1]
    p = _linear(u, w_in)
    ys = [_gla_scan(p, lb[dr], d, n_ctx, dr == 1) for dr in range(2)]
    return _gla_post(ys[0], ys[1], p, norm_w)


def _rglru_mixer(u, n_ctx, w_in, conv_w, conv_b, wa, ba, wx, bx, lam):
    log_sig = jax.nn.log_sigmoid(lam)
    p = _linear(u, w_in)
    w = p.shape[1] // 2
    xc, xcb = _conv(p, w, conv_w, conv_b, n_ctx, False, (F32, BF16))
    hs = []
    for dr in range(2):
        ra = _block_diag(xcb, wa[dr].astype(BF16))
        rx = _block_diag(xcb, wx[dr].astype(BF16))
        hs.append(_lru_scan(ra, rx, xc, ba[dr], bx[dr], log_sig[dr], n_ctx, dr == 1))
    return _lru_post(hs[0], hs[1], p)


def _mamba2_mixer(u, n_ctx, w_in, conv_w, conv_b, dt_bias, a_log, d_skip, norm_w):
    ts, d = u.shape
    di = 2 * d
    gn = M2_G * M2_N
    hh = di // M2_P
    zx = _linear(u, w_in, n_out=di + di + 2 * gn)
    dt_raw = _linear(u, w_in, col0=di + di + 2 * gn, n_out=2 * hh)
    xbc, = _conv(zx, di, conv_w, conv_b, n_ctx, True, (F32,))
    a_neg = -jnp.exp(a_log)
    dt = jax.nn.softplus(dt_raw.reshape(ts, 2, hh) + dt_bias)
    ys = [_ssd_scan(xbc, di, dt[:, dr], dt[:, dr] * a_neg[dr], n_ctx, dr == 1) for dr in range(2)]
    return _ssd_post(ys[0], ys[1], xbc, zx, d_skip, norm_w)


def _gather_body(tok_ref, x_hbm, o_ref, buf_ref, sem, *, tg):
    base = pl.program_id(0) * tg

    def row_copy(src_row, r):
        return pltpu.make_async_copy(x_hbm.at[pl.ds(src_row, 1), :], buf_ref.at[pl.ds(r, 1), :], sem)

    def start(r, carry):
        row_copy(tok_ref[base + r], r).start()
        return carry

    def wait(r, carry):
        row_copy(0, r).wait()
        return carry

    lax.fori_loop(0, tg, start, 0)
    lax.fori_loop(0, tg, wait, 0)
    o_ref[...] = buf_ref[...].astype(o_ref.dtype)


def _gather_rows(x, tok, tg):
    p, d = tok.shape[0], x.shape[1]
    return pl.pallas_call(
        functools.partial(_gather_body, tg=tg),
        grid_spec=pltpu.PrefetchScalarGridSpec(
            num_scalar_prefetch=1, grid=(p // tg,),
            in_specs=[pl.BlockSpec(memory_space=pl.ANY)],
            out_specs=pl.BlockSpec((tg, d), lambda i, tok: (i, 0)),
            scratch_shapes=[pltpu.VMEM((tg, d), x.dtype), pltpu.SemaphoreType.DMA(())]),
        out_shape=jax.ShapeDtypeStruct((p, d), BF16),
        compiler_params=_params(1), name="moe_gather",
    )(tok, x)


def _moe_up_body(te_ref, x_ref, wg_ref, wu_ref, o_ref):
    x = x_ref[...]
    g = jnp.dot(x, wg_ref[...], preferred_element_type=F32)
    u = jnp.dot(x, wu_ref[...], preferred_element_type=F32)
    o_ref[...] = (g * jax.nn.sigmoid(g) * u).astype(o_ref.dtype)


def _moe_up_sparse(xg, w_gu, tile_e, tg):
    p, d = xg.shape
    f = w_gu.shape[2] // 2
    tn = _tile(f, (512, 256, 128))
    nj = f // tn
    return pl.pallas_call(
        _moe_up_body,
        grid_spec=pltpu.PrefetchScalarGridSpec(
            num_scalar_prefetch=1, grid=(nj, p // tg),
            in_specs=[pl.BlockSpec((tg, d), lambda j, i, te: (i, 0)),
                      pl.BlockSpec((None, d, tn), lambda j, i, te: (te[i], 0, j)),
                      pl.BlockSpec((None, d, tn), lambda j, i, te: (te[i], 0, j + nj))],
            out_specs=pl.BlockSpec((tg, tn), lambda j, i, te: (i, j))),
        out_shape=jax.ShapeDtypeStruct((p, f), BF16),
        compiler_params=_params(2), name="moe_up",
    )(tile_e, xg, w_gu, w_gu)


def _moe_down_body(te_ref, a_ref, w_ref, ws_ref, o_ref):
    o_ref[...] = jnp.dot(a_ref[...], w_ref[...], preferred_element_type=F32) * ws_ref[...]


def _moe_down_sparse(act, w_down, wslot, tile_e, tg):
    p, f = act.shape
    d = w_down.shape[2]
    tn = _tile(d, (1024, 512, 256, 128))
    return pl.pallas_call(
        _moe_down_body,
        grid_spec=pltpu.PrefetchScalarGridSpec(
            num_scalar_prefetch=1, grid=(d // tn, p // tg),
            in_specs=[pl.BlockSpec((tg, f), lambda j, i, te: (i, 0)),
                      pl.BlockSpec((None, f, tn), lambda j, i, te: (te[i], 0, j)),
                      pl.BlockSpec((tg, 1), lambda j, i, te: (i, 0))],
            out_specs=pl.BlockSpec((tg, tn), lambda j, i, te: (i, j))),
        out_shape=jax.ShapeDtypeStruct((p, d), F32),
        compiler_params=_params(2), name="moe_down",
    )(tile_e, act, w_down, wslot)


def _combine_body(slot_ref, y_hbm, res_ref, g_ref, o_ref, buf_ref, sem, *, tm, n_ctx):
    base = pl.program_id(0) * tm

    def row_copy(src_row, k, r):
        return pltpu.make_async_copy(y_hbm.at[pl.ds(src_row, 1), :], buf_ref.at[k, pl.ds(r, 1), :], sem)

    def start(r, carry):
        for k in range(TOP_K):
            row_copy(slot_ref[TOP_K * (base + r) + k], k, r).start()
        return carry

    def wait(r, carry):
        for k in range(TOP_K):
            row_copy(0, k, r).wait()
        return carry

    lax.fori_loop(0, tm, start, 0)
    lax.fori_loop(0, tm, wait, 0)
    row = base + lax.broadcasted_iota(jnp.int32, (tm, 1), 0)
    gate = jnp.where(row < n_ctx, g_ref[0:1, :], g_ref[1:2, :])
    o_ref[...] = res_ref[...] + gate * (buf_ref[0] + buf_ref[1])


def _moe_combine(y, slot, res, gate, n_ctx):
    m, d = res.shape
    tm = _tile(m, (256, 128, 64, 8))
    return pl.pallas_call(
        functools.partial(_combine_body, tm=tm, n_ctx=n_ctx),
        grid_spec=pltpu.PrefetchScalarGridSpec(
            num_scalar_prefetch=1, grid=(m // tm,),
            in_specs=[pl.BlockSpec(memory_space=pl.ANY),
                      pl.BlockSpec((tm, d), lambda i, sl: (i, 0)),
                      pl.BlockSpec((2, d), lambda i, sl: (0, 0))],
            out_specs=pl.BlockSpec((tm, d), lambda i, sl: (i, 0)),
            scratch_shapes=[pltpu.VMEM((TOP_K, tm, d), F32), pltpu.SemaphoreType.DMA(())]),
        out_shape=jax.ShapeDtypeStruct((m, d), F32),
        compiler_params=_params(1), name="moe_combine",
    )(slot, y, res, gate)


def _moe(v32, router, w_gu, w_down, res, gate, n_ctx):
    m = v32.shape[0]
    e = router.shape[1]
    n_ent = TOP_K * m
    tg = _tile(n_ent, (256, 128, 64, 8))
    logits = _linear(v32, router, precision=lax.Precision.HIGHEST)
    top_val, top_idx = lax.top_k(logits, TOP_K)
    gates = jax.nn.softmax(top_val, axis=-1)
    ex = top_idx.reshape(-1)
    cnt = jnp.sum(jax.nn.one_hot(ex, e, dtype=jnp.int32), axis=0)
    padded = (cnt + tg - 1) // tg * tg
    ends = jnp.cumsum(padded)
    order = jnp.argsort(ex, stable=True)
    ex_sorted = ex[order]
    rank = jnp.arange(n_ent, dtype=jnp.int32) - (jnp.cumsum(cnt) - cnt)[ex_sorted]
    slot = jnp.zeros((n_ent,), jnp.int32).at[order].set((ends - padded)[ex_sorted] + rank)
    n_slot = n_ent + e * tg
    tok = jnp.zeros((n_slot,), jnp.int32).at[slot].set(jnp.arange(n_ent, dtype=jnp.int32) // TOP_K)
    wslot = jnp.zeros((n_slot,), F32).at[slot].set(gates.reshape(-1)).reshape(n_slot, 1)
    tile_start = jnp.arange(n_slot // tg, dtype=jnp.int32) * tg
    tile_e = jnp.minimum(jnp.sum(tile_start[:, None] >= ends[None, :], axis=1), e - 1).astype(jnp.int32)

    xg = _gather_rows(v32, tok, tg)
    act = _moe_up_sparse(xg, w_gu.astype(BF16), tile_e, tg)
    y = _moe_down_sparse(act, w_down.astype(BF16), wslot, tile_e, tg)
    return _moe_combine(y, slot, res, gate, n_ctx)


def kernel(x, c, ctx, c_ctx, mod_down, mod_up, mod_b, norm_w, final_norm_w, rw_mix, rw_w_rkv, rw_w0, rw_w1, rw_w2, rw_a0, rw_a1, rw_a2, rw_g1, rw_g2, rw_k_k, rw_k_a, rw_r_k, rw_lnx_w, rw_lnx_b, rw_w_o, hg_w_in, hg_lb, hg_norm_w, hg_w_o, rg_w_in, rg_conv_w, rg_conv_b, rg_wa, rg_ba, rg_wx, rg_bx, rg_lam, rg_w_out, m2_w_in, m2_conv_w, m2_conv_b, m2_dt_bias, m2_a_log, m2_d, m2_norm_w, m2_w_out, ffn_w_gu, ffn_w_down, moe_router, moe_w_gu, moe_w_down):
    assert x.shape[0] == 1 and c.shape[0] == 1
    depth = mod_down.shape[0]
    n_ctx, d = ctx.shape[1], ctx.shape[2]
    xh = jnp.concatenate([ctx[0], x[0]], axis=0)
    cvec = jnp.zeros((8, d), F32).at[0].set(c_ctx).at[1].set(c[0])
    cvec = jax.nn.silu(cvec).astype(BF16)
    for i in range(depth):
        kind = i % 4
        col_major = i % 2 == 1
        low = _linear(cvec, mod_down[i]).astype(BF16)
        mod = (_linear(low, mod_up[i])[:2] + mod_b[i]).reshape(2, 6, d)

        want = (F32,) if kind == 0 else (BF16,)
        u = _to_scan(_norm_mod(xh, norm_w[i, 0], mod, 0, 1, n_ctx, want)[0], n_ctx, col_major)
        if kind == 0:
            o = _rwkv7_mixer(u, n_ctx, rw_mix, rw_w_rkv, rw_w0, rw_w1, rw_w2, rw_a0, rw_a1, rw_a2,
                             rw_g1, rw_g2, rw_k_k, rw_k_a, rw_r_k, rw_lnx_w, rw_lnx_b)
            w_out = rw_w_o
        elif kind == 1:
            o = _hgrn2_mixer(u, n_ctx, i, hg_w_in, hg_lb, hg_norm_w)
            w_out = hg_w_o
        elif kind == 2:
            o = _rglru_mixer(u, n_ctx, rg_w_in, rg_conv_w, rg_conv_b, rg_wa, rg_ba, rg_wx, rg_bx, rg_lam)
            w_out = rg_w_out
        else:
            o = _mamba2_mixer(u, n_ctx, m2_w_in, m2_conv_w, m2_conv_b, m2_dt_bias, m2_a_log, m2_d, m2_norm_w)
            w_out = m2_w_out
        xh = _linear_residual(_to_grid(o, n_ctx, col_major), _wt(w_out), xh, mod[:, 2], n_ctx)

        fi = i // 2
        if i % 2 == 0:
            vb, = _norm_mod(xh, norm_w[i, 1], mod, 4, 3, n_ctx, (BF16,))
            act = _swiglu_up(vb, ffn_w_gu[fi].astype(BF16))
            xh = _linear_residual(act, ffn_w_down[fi].astype(BF16), xh, mod[:, 5], n_ctx)
        else:
            v32, = _norm_mod(xh, norm_w[i, 1], mod, 4, 3, n_ctx, (F32,))
            xh = _moe(v32, moe_router[fi], moe_w_gu[fi], moe_w_down[fi], xh, mod[:, 5], n_ctx)
    out = _rmsnorm(xh[n_ctx:], final_norm_w, F32)
    return out[None]
```

```python
import functools
import math

import jax
import jax.numpy as jnp
from jax import lax
from jax.experimental import pallas as pl
from jax.experimental.pallas import tpu as pltpu

F32 = jnp.float32
BF16 = jnp.bfloat16

NORM_EPS = 1e-6
GRID_W = 64
CHUNK = 64
RW_N = 64
RW_GN_EPS = 64e-5
HG_K = 128
RG_H = 16
RG_C = 8.0
M2_P = 64
M2_G = 8
M2_N = 128
TOP_K = 2

_VMEM_LIMIT = 56 * 1024 * 1024

_NN = (((1,), (0,)), ((), ()))
_NT = (((1,), (1,)), ((), ()))
_TN = (((0,), (0,)), ((), ()))


def _params(n_axes):
    return pltpu.CompilerParams(dimension_semantics=("arbitrary",) * n_axes,
                                vmem_limit_bytes=_VMEM_LIMIT)


def _tile(n, prefs):
    for p in prefs:
        if n % p == 0:
            return p
    return n


def _bdot(a, b, dims=_NN):
    return lax.dot_general(a.astype(BF16), b.astype(BF16), dims, preferred_element_type=F32)


def _split3(x):
    hi = x.astype(BF16)
    r1 = x - hi.astype(F32)
    mid = r1.astype(BF16)
    lo = (r1 - mid.astype(F32)).astype(BF16)
    return hi, mid, lo


def _cumsum_rows(tri, x):
    hi, mid, lo = _split3(x)
    d = lambda p: jnp.dot(tri, p, preferred_element_type=F32)
    return d(hi) + d(mid) + d(lo)


def _cumsum_cols(x, triu):
    hi, mid, lo = _split3(x)
    d = lambda p: jnp.dot(p, triu, preferred_element_type=F32)
    return d(hi) + d(mid) + d(lo)


def _scan_masks(c, rev):
    ti = lax.broadcasted_iota(jnp.int32, (c, c), 0)
    si = lax.broadcasted_iota(jnp.int32, (c, c), 1)
    upto = (ti <= si) if rev else (ti >= si)
    before = (ti < si) if rev else (ti > si)
    eye = jnp.where(ti == si, 1.0, 0.0)
    return upto, before, eye, (0 if rev else c - 1)


def _block_order(i, n_first, n_total, rev):
    if not rev:
        return i
    return jnp.where(i < n_first, n_first - 1 - i, n_total + n_first - 1 - i)


def _norm_mod_body(x_ref, w_ref, mod_ref, *o_refs, shift_i, scale_i, n_ctx, tm):
    x = x_ref[...]
    y = x * lax.rsqrt(jnp.mean(x * x, axis=-1, keepdims=True) + NORM_EPS) * w_ref[...]
    row = pl.program_id(0) * tm + lax.broadcasted_iota(jnp.int32, (tm, 1), 0)
    is_ctx = row < n_ctx
    shift = jnp.where(is_ctx, mod_ref[0, shift_i:shift_i + 1, :], mod_ref[1, shift_i:shift_i + 1, :])
    scale = jnp.where(is_ctx, mod_ref[0, scale_i:scale_i + 1, :], mod_ref[1, scale_i:scale_i + 1, :])
    y = y * (1.0 + scale) + shift
    for o in o_refs:
        o[...] = y.astype(o.dtype)


def _norm_mod(xh, w, mod, shift_i, scale_i, n_ctx, dtypes):
    m, d = xh.shape
    tm = _tile(m, (256, 128, 64, 8))
    out = pl.pallas_call(
        functools.partial(_norm_mod_body, shift_i=shift_i, scale_i=scale_i, n_ctx=n_ctx, tm=tm),
        grid=(m // tm,),
        in_specs=[pl.BlockSpec((tm, d), lambda i: (i, 0)),
                  pl.BlockSpec((1, d), lambda i: (0, 0)),
                  pl.BlockSpec((2, 6, d), lambda i: (0, 0, 0))],
        out_specs=[pl.BlockSpec((tm, d), lambda i: (i, 0)) for _ in dtypes],
        out_shape=[jax.ShapeDtypeStruct((m, d), dt) for dt in dtypes],
        compiler_params=_params(1), name="norm_mod",
    )(xh, w.reshape(1, d), mod)
    return out


def _shift_mix_body(x_ref, pv_ref, nx_ref, w_ref, mod_ref, mix_ref, *o_refs, n_ctx, ts, tm):
    row0 = pl.program_id(0) * tm
    grp = jnp.where(row0 < n_ctx, 0, 1)
    shift, scale = mod_ref[grp, 0:1, :], mod_ref[grp, 1:2, :]

    def norm(x):
        y = x * lax.rsqrt(jnp.mean(x * x, axis=-1, keepdims=True) + NORM_EPS) * w_ref[...]
        return y * (1.0 + scale) + shift

    at_start = jnp.logical_or(row0 == 0, row0 == n_ctx)
    at_end = jnp.logical_or(row0 + tm == n_ctx, row0 + tm == ts)
    u = norm(x_ref[...])
    pv = jnp.where(at_start, 0.0, norm(pv_ref[...]))
    nx = jnp.where(at_end, 0.0, norm(nx_ref[...]))
    r = lax.broadcasted_iota(jnp.int32, (tm, 1), 0)
    um1 = jnp.where(r >= 1, pltpu.roll(u, 1, 0), pv[7:8])
    up1 = jnp.where(r <= tm - 2, pltpu.roll(u, tm - 1, 0), nx[0:1])
    xx = 0.5 * (um1 + up1) - u
    for j, o in enumerate(o_refs):
        o[...] = (u + xx * mix_ref[j:j + 1, :]).astype(o.dtype)


def _shift_mix(xh, w, mod, mix, n_ctx):
    ts, d = xh.shape
    n = mix.shape[0]
    tm = _tile(math.gcd(ts, n_ctx), (256, 128, 64, 8))
    r8, last8 = tm // 8, ts // 8 - 1
    row = pl.BlockSpec((tm, d), lambda i: (i, 0))
    return pl.pallas_call(
        functools.partial(_shift_mix_body, n_ctx=n_ctx, ts=ts, tm=tm),
        grid=(ts // tm,),
        in_specs=[row,
                  pl.BlockSpec((8, d), lambda i: (jnp.maximum(i * r8 - 1, 0), 0)),
                  pl.BlockSpec((8, d), lambda i: (jnp.minimum((i + 1) * r8, last8), 0)),
                  pl.BlockSpec((1, d), lambda i: (0, 0)),
                  pl.BlockSpec((2, 6, d), lambda i: (0, 0, 0)),
                  pl.BlockSpec((n, d), lambda i: (0, 0))],
        out_specs=[row for _ in range(n)],
        out_shape=[jax.ShapeDtypeStruct((ts, d), BF16) for _ in range(n)],
        compiler_params=_params(1), name="shift_mix",
    )(xh, xh, xh, w.reshape(1, d), mod, mix)


def _rmsnorm_body(x_ref, w_ref, o_ref, *, groups):
    x = x_ref[...]
    gw = x.shape[-1] // groups
    for g in range(groups):
        xs = x[:, g * gw:(g + 1) * gw]
        o_ref[:, g * gw:(g + 1) * gw] = (
            xs * lax.rsqrt(jnp.mean(xs * xs, axis=-1, keepdims=True) + NORM_EPS)
            * w_ref[:, g * gw:(g + 1) * gw]).astype(o_ref.dtype)


def _rmsnorm(x, w, out_dtype, groups=1):
    m, d = x.shape
    tm = _tile(m, (256, 128, 64, 8))
    return pl.pallas_call(
        functools.partial(_rmsnorm_body, groups=groups),
        grid=(m // tm,),
        in_specs=[pl.BlockSpec((tm, d), lambda i: (i, 0)), pl.BlockSpec((1, d), lambda i: (0, 0))],
        out_specs=pl.BlockSpec((tm, d), lambda i: (i, 0)),
        out_shape=jax.ShapeDtypeStruct((m, d), out_dtype),
        compiler_params=_params(1), name="rmsnorm",
    )(x, w.reshape(1, d))


def _mm_body(*refs, nw, nex, nk, epi, precision):
    x_ref = refs[0]
    w_refs = refs[1:1 + nw]
    ex_refs = refs[1 + nw:1 + nw + nex]
    o_ref = refs[1 + nw + nex]
    acc_refs = refs[2 + nw + nex:]
    x = x_ref[...]
    prods = [jnp.dot(x, w[...] if precision is not None else w[...].astype(x.dtype),
                     preferred_element_type=F32, precision=precision) for w in w_refs]

    def finish(accs):
        o_ref[...] = epi(accs, [e[...] for e in ex_refs], pl.program_id(1)).astype(o_ref.dtype)

    if nk == 1:
        finish(prods)
    else:
        k = pl.program_id(2)

        @pl.when(k == 0)
        def _():
            for a, p in zip(acc_refs, prods):
                a[...] = p

        @pl.when(k > 0)
        def _():
            for a, p in zip(acc_refs, prods):
                a[...] += p

        @pl.when(k == nk - 1)
        def _():
            finish([a[...] for a in acc_refs])


def _matmul(x, ws, w_specs, *, n_out, epi, tm, tn, tk=None, extras=(), extra_specs=(),
            out_dtype=F32, x_map=None, k_total=None, precision=None, name="mm"):
    m = x.shape[0]
    k_total = k_total or x.shape[1]
    tk = tk or k_total
    nk = k_total // tk
    x_map = x_map or (lambda j, i, k: (i, k))
    return pl.pallas_call(
        functools.partial(_mm_body, nw=len(ws), nex=len(extras), nk=nk, epi=epi, precision=precision),
        grid=(n_out // tn, m // tm, nk),
        in_specs=[pl.BlockSpec((tm, tk), x_map)] + list(w_specs) + list(extra_specs),
        out_specs=pl.BlockSpec((tm, tn), lambda j, i, k: (i, j)),
        out_shape=jax.ShapeDtypeStruct((m, n_out), out_dtype),
        scratch_shapes=[pltpu.VMEM((tm, tn), F32) for _ in ws] if nk > 1 else [],
        compiler_params=_params(3), name=name,
    )(x, *ws, *extras)


def _wt(w):
    return w if w.shape[-2] <= 4096 else w.astype(BF16)


def _epi_plain(accs, extras, i):
    return accs[0]


def _row_tile(m):
    return _tile(m, (768, 512, 640, 384, 256, 128, 64, 8))


def _k_tile(k):
    return k if k <= 4096 else _tile(k, (4096, 2048, 1024))


def _linear(x, w, out_dtype=F32, col0=0, n_out=None, precision=None):
    m, k = x.shape
    n_out = n_out or w.shape[1]
    tn = _tile(math.gcd(n_out, col0) if col0 else n_out, (512, 256, 128))
    tn = min(tn, n_out)
    c0 = col0 // tn
    tk = _k_tile(k)
    return _matmul(x, [w], [pl.BlockSpec((tk, tn), lambda j, i, kk: (kk, j + c0))],
                   n_out=n_out, epi=_epi_plain, tm=_row_tile(m), tn=tn, tk=tk,
                   out_dtype=out_dtype, precision=precision)


def _linear_residual(x, w, res, gate, n_ctx):
    m, k = x.shape
    n = w.shape[1]
    tm, tn, tk = _row_tile(m), _tile(n, (512, 256, 128)), _k_tile(k)

    def epi(accs, extras, i):
        r, g = extras
        row = i * tm + lax.broadcasted_iota(jnp.int32, (tm, 1), 0)
        return r + jnp.where(row < n_ctx, g[0:1, :], g[1:2, :]) * accs[0]

    return _matmul(x, [w], [pl.BlockSpec((tk, tn), lambda j, i, kk: (kk, j))],
                   n_out=n, epi=epi, tm=tm, tn=tn, tk=tk, extras=(res, gate),
                   extra_specs=(pl.BlockSpec((tm, tn), lambda j, i, kk: (i, j)),
                                pl.BlockSpec((2, tn), lambda j, i, kk: (0, j))))


def _epi_swiglu(accs, extras, i):
    g, u = accs
    return g * jax.nn.sigmoid(g) * u


def _swiglu_up(x, w_gu):
    m, k = x.shape
    f = w_gu.shape[1] // 2
    tn = _tile(f, (512, 256, 128))
    nj = f // tn
    return _matmul(x, [w_gu, w_gu],
                   [pl.BlockSpec((k, tn), lambda j, i, kk: (0, j)),
                    pl.BlockSpec((k, tn), lambda j, i, kk: (0, j + nj))],
                   n_out=f, epi=_epi_swiglu, tm=_row_tile(m), tn=tn, out_dtype=BF16)


def _block_diag(x, w):
    m = x.shape[0]
    h, bw, _ = w.shape
    return _matmul(x, [w], [pl.BlockSpec((None, bw, bw), lambda j, i, kk: (j, 0, 0))],
                   n_out=h * bw, epi=_epi_plain, tm=_row_tile(m), tn=bw, tk=bw,
                   x_map=lambda j, i, kk: (i, j), k_total=bw)


def _rwkv_body(r_ref, k_ref, v_ref, kk_ref, zw_ref, za_ref, w0_ref, a0_ref, ka_ref, o_ref, s_ref, *, heads, n, rev):
    c = CHUNK

    @pl.when(pl.program_id(1) == 0)
    def _():
        s_ref[...] = jnp.zeros_like(s_ref)

    upto, before, eye, row_last = _scan_masks(c, rev)
    tri = jnp.where(upto, 1.0, 0.0).astype(BF16)
    lw = -math.exp(-0.5) * jax.nn.sigmoid(w0_ref[...] + zw_ref[...])
    a_gate = jax.nn.sigmoid(a0_ref[...] + za_ref[...])
    kk = kk_ref[...]
    kd = k_ref[...] * (1.0 + (a_gate - 1.0) * ka_ref[...])
    b = kk * a_gate
    cum = _cumsum_rows(tri, lw)
    last = cum[row_last:row_last + 1, :]
    g_inv = jnp.exp(-cum)
    tail = jnp.exp(last - cum)
    dec = jnp.exp(last)
    at = (-kk * jnp.exp(cum - lw)).astype(BF16)
    rt = (r_ref[...] * jnp.exp(cum)).astype(BF16)
    bt, kt = (b * g_inv).astype(BF16), (kd * g_inv).astype(BF16)
    bg, kg = (b * tail).astype(BF16), (kd * tail).astype(BF16)
    vb = v_ref[...].astype(BF16)
    hs = range(heads)
    sl = [slice(h * n, (h + 1) * n) for h in hs]
    lhs = [jnp.concatenate([at[:, s], rt[:, s]], axis=0) for s in sl]
    rhs = [jnp.concatenate([bt[:, s], kt[:, s]], axis=0) for s in sl]
    s0 = [s_ref[h] for h in hs]
    p = [_bdot(lhs[h], rhs[h], _NT) for h in hs]
    lm = [_bdot(lhs[h], s0[h], _NT) for h in hs]
    x = [jnp.where(before, p[h][:c, :c], 0.0) for h in hs]
    t = [eye + xh for xh in x]
    for _ in range(int(math.log2(c)) - 1):
        x = [_bdot(xh, xh) for xh in x]
        t = [th + _bdot(th, xh) for th, xh in zip(t, x)]
    w0 = [lm[h][:c] + _bdot(jnp.where(before, p[h][:c, c:], 0.0), vb[:, sl[h]]) for h in hs]
    u = [_bdot(t[h], w0[h]) for h in hs]
    uv = [jnp.concatenate([u[h].astype(BF16), vb[:, sl[h]]], axis=0) for h in hs]
    for h in hs:
        r_bk = jnp.concatenate([jnp.where(upto, p[h][c:, :c], 0.0), jnp.where(upto, p[h][c:, c:], 0.0)], axis=1)
        o_ref[:, sl[h]] = lm[h][c:] + _bdot(r_bk, uv[h])
    for h in hs:
        bk = jnp.concatenate([bg[:, sl[h]], kg[:, sl[h]]], axis=0)
        s_ref[h] = s0[h] * dec[:, sl[h]] + _bdot(uv[h], bk, _TN)


def _rwkv_scan(r, k, v, kk, zw, za, w0, a0, k_a, n_ctx, rev):
    t, d = r.shape
    heads = 16 if d % (16 * RW_N) == 0 else d // RW_N
    w = heads * RW_N
    nch, ncc = t // CHUNK, n_ctx // CHUNK
    spec = pl.BlockSpec((CHUNK, w), lambda g, c: (_block_order(c, ncc, nch, rev), g))
    pspec = pl.BlockSpec((1, w), lambda g, c: (0, g))
    return pl.pallas_call(
        functools.partial(_rwkv_body, heads=heads, n=RW_N, rev=rev),
        grid=(d // w, nch),
        in_specs=[spec] * 6 + [pspec] * 3,
        out_specs=spec,
        out_shape=jax.ShapeDtypeStruct((t, d), F32),
        scratch_shapes=[pltpu.VMEM((heads, RW_N, RW_N), F32)],
        compiler_params=_params(2), name="rwkv_rev" if rev else "rwkv_fwd",
    )(r, k, v, kk, zw, za, w0.reshape(1, d), a0.reshape(1, d), k_a.reshape(1, d))


def _gla_body(q_ref, zf_ref, v_ref, lb_ref, o_ref, s_ref, *, heads, kdim, sub, rev):
    c = CHUNK

    @pl.when(pl.program_id(1) == 0)
    def _():
        s_ref[...] = jnp.zeros_like(s_ref)

    upto, _, _, row_last = _scan_masks(c, rev)
    tri = jnp.where(upto, 1.0, 0.0).astype(BF16)
    lb, zf = lb_ref[...], zf_ref[...]
    cum = _cumsum_rows(tri, jnp.log(lb + (1.0 - lb) * jax.nn.sigmoid(zf)))
    last = cum[row_last:row_last + 1, :]
    k = (1.0 - lb) * jax.nn.sigmoid(-zf)
    q = q_ref[...]
    q = q * jax.nn.sigmoid(q)
    vb = v_ref[...].astype(BF16)
    qe = (q * jnp.exp(cum)).astype(BF16)
    ktail = (k * jnp.exp(last - cum)).astype(BF16)
    dec = jnp.exp(last)
    hs = range(heads)
    sl = [slice(h * kdim, (h + 1) * kdim) for h in hs]
    s0 = [s_ref[h] for h in hs]
    inter = [_bdot(qe[:, sl[h]], s0[h], _NT) for h in hs]
    nb = c // sub
    for i in range(nb):
        lo, hi = i * sub, (i + 1) * sub
        if rev:
            ks = slice(lo, c)
            piv = cum[hi:hi + 1, :] if i < nb - 1 else jnp.zeros_like(last)
        else:
            ks = slice(0, hi)
            piv = cum[lo - 1:lo, :] if i > 0 else jnp.zeros_like(last)
        nk = ks.stop - ks.start
        qi = (q[lo:hi] * jnp.exp(cum[lo:hi] - piv)).astype(BF16)
        ki = (k[ks] * jnp.exp(piv - cum[ks])).astype(BF16)
        tt = lo + lax.broadcasted_iota(jnp.int32, (sub, nk), 0)
        ss = ks.start + lax.broadcasted_iota(jnp.int32, (sub, nk), 1)
        mask = (tt <= ss) if rev else (tt >= ss)
        att = [jnp.where(mask, _bdot(qi[:, sl[h]], ki[:, sl[h]], _NT), 0.0) for h in hs]
        for h in hs:
            o_ref[lo:hi, sl[h]] = inter[h][lo:hi] + _bdot(att[h], vb[ks, sl[h]])
    for h in hs:
        s_ref[h] = s0[h] * dec[:, sl[h]] + _bdot(vb[:, sl[h]], ktail[:, sl[h]], _TN)


def _gla_scan(p, lb, d, n_ctx, rev):
    t = p.shape[0]
    heads = 8 if d % (8 * HG_K) == 0 else d // HG_K
    w = heads * HG_K
    nd = d // w
    nch, ncc = t // CHUNK, n_ctx // CHUNK
    zf_col = (4 if rev else 3) * nd

    def spec(col):
        return pl.BlockSpec((CHUNK, w), lambda gi, c: (_block_order(c, ncc, nch, rev), col + gi))

    return pl.pallas_call(
        functools.partial(_gla_body, heads=heads, kdim=HG_K, sub=16, rev=rev),
        grid=(nd, nch),
        in_specs=[spec(0), spec(zf_col), spec(nd), pl.BlockSpec((1, w), lambda gi, c: (0, gi))],
        out_specs=spec(0),
        out_shape=jax.ShapeDtypeStruct((t, d), F32),
        scratch_shapes=[pltpu.VMEM((heads, HG_K, HG_K), F32)],
        compiler_params=_params(2), name="gla_rev" if rev else "gla_fwd",
    )(p, p, p, lb.reshape(1, d))


def _gla_post_body(yf_ref, yr_ref, og_ref, w_ref, o_ref):
    y = yf_ref[...] + yr_ref[...]
    og = og_ref[...]
    y = y * lax.rsqrt(jnp.mean(y * y, axis=-1, keepdims=True) + NORM_EPS) * w_ref[...]
    o_ref[...] = (y * (og * jax.nn.sigmoid(og))).astype(o_ref.dtype)


def _gla_post(y_f, y_r, p, norm_w):
    t, d = y_f.shape
    tm = _tile(t, (256, 128, 64, 8))
    row = pl.BlockSpec((tm, d), lambda i: (i, 0))
    return pl.pallas_call(
        _gla_post_body,
        grid=(t // tm,),
        in_specs=[row, row, pl.BlockSpec((tm, d), lambda i: (i, 2)), pl.BlockSpec((1, d), lambda i: (0, 0))],
        out_specs=row,
        out_shape=jax.ShapeDtypeStruct((t, d), BF16),
        compiler_params=_params(1), name="gla_post",
    )(y_f, y_r, p, norm_w.reshape(1, d))


def _conv_body(x_ref, pv_ref, nx_ref, w_ref, b_ref, *o_refs, tm, n_ctx, ts, silu):
    row0 = pl.program_id(0) * tm
    x = x_ref[...]
    at_start = jnp.logical_or(row0 == 0, row0 == n_ctx)
    at_end = jnp.logical_or(row0 + tm == n_ctx, row0 + tm == ts)
    pv = jnp.where(at_start, 0.0, pv_ref[...])
    nx = jnp.where(at_end, 0.0, nx_ref[...])
    r = lax.broadcasted_iota(jnp.int32, (tm, 1), 0)
    xm1 = jnp.where(r >= 1, pltpu.roll(x, 1, 0), pv[7:8])
    xm2 = jnp.where(r >= 2, pltpu.roll(x, 2, 0), jnp.where(r == 1, pv[7:8], pv[6:7]))
    xp1 = jnp.where(r <= tm - 2, pltpu.roll(x, tm - 1, 0), nx[0:1])
    w = w_ref[...]
    y = w[0:1] * xm2 + w[1:2] * xm1 + w[2:3] * x + w[3:4] * xp1 + b_ref[...]
    if silu:
        y = y * jax.nn.sigmoid(y)
    for o in o_refs:
        o[...] = y.astype(o.dtype)


def _conv(x, col0, w, b, n_ctx, silu, dtypes):
    ts = x.shape[0]
    cw = w.shape[1]
    tm = _tile(math.gcd(ts, n_ctx), (256, 128, 64, 8))
    tc = _tile(math.gcd(cw, col0) if col0 else cw, (512, 256, 128))
    c0, r8, last8 = col0 // tc, tm // 8, ts // 8 - 1
    out = pl.BlockSpec((tm, tc), lambda i, j: (i, j))
    return pl.pallas_call(
        functools.partial(_conv_body, tm=tm, n_ctx=n_ctx, ts=ts, silu=silu),
        grid=(ts // tm, cw // tc),
        in_specs=[pl.BlockSpec((tm, tc), lambda i, j: (i, j + c0)),
                  pl.BlockSpec((8, tc), lambda i, j: (jnp.maximum(i * r8 - 1, 0), j + c0)),
                  pl.BlockSpec((8, tc), lambda i, j: (jnp.minimum((i + 1) * r8, last8), j + c0)),
                  pl.BlockSpec((4, tc), lambda i, j: (0, j)),
                  pl.BlockSpec((1, tc), lambda i, j: (0, j))],
        out_specs=[out for _ in dtypes],
        out_shape=[jax.ShapeDtypeStruct((ts, cw), dt) for dt in dtypes],
        compiler_params=_params(2), name="conv",
    )(x, x, x, w, b.reshape(1, cw))


def _lru_body(ra_ref, rx_ref, xc_ref, ba_ref, bx_ref, ls_ref, o_ref, h_ref, a_ref, u_ref, *, tb, rev):
    @pl.when(pl.program_id(1) == 0)
    def _():
        h_ref[...] = jnp.zeros_like(h_ref)

    r = jax.nn.sigmoid(ra_ref[...] + ba_ref[...])
    ig = jax.nn.sigmoid(rx_ref[...] + bx_ref[...])
    log_a = RG_C * r * ls_ref[...]
    a = jnp.exp(log_a)
    a_ref[...] = a
    u_ref[...] = jnp.sqrt(-jnp.tanh(log_a) * (a * a + 1.0)) * ig * xc_ref[...]

    def step(i, h):
        t = tb - 1 - i if rev else i
        h = a_ref[pl.ds(t, 1), :] * h + u_ref[pl.ds(t, 1), :]
        o_ref[pl.ds(t, 1), :] = h
        return h

    h_ref[0:1, :] = lax.fori_loop(0, tb, step, h_ref[0:1, :], unroll=8)


def _lru_scan(ra, rx, xc, ba, bx, log_sig, n_ctx, rev):
    t, w = ra.shape
    tb = _tile(math.gcd(t, n_ctx), (256, 128, 64, 8))
    lb = _tile(w, (512, 256, 128))
    nb, nbc = t // tb, n_ctx // tb
    spec = pl.BlockSpec((tb, lb), lambda j, i: (_block_order(i, nbc, nb, rev), j))
    pspec = pl.BlockSpec((1, lb), lambda j, i: (0, j))
    return pl.pallas_call(
        functools.partial(_lru_body, tb=tb, rev=rev),
        grid=(w // lb, nb),
        in_specs=[spec, spec, spec, pspec, pspec, pspec],
        out_specs=spec,
        out_shape=jax.ShapeDtypeStruct((t, w), F32),
        scratch_shapes=[pltpu.VMEM((8, lb), F32), pltpu.VMEM((tb, lb), F32), pltpu.VMEM((tb, lb), F32)],
        compiler_params=_params(2), name="lru_rev" if rev else "lru_fwd",
    )(ra, rx, xc, ba.reshape(1, w), bx.reshape(1, w), log_sig.reshape(1, w))


def _lru_post_body(hf_ref, hr_ref, g_ref, o_ref):
    o_ref[...] = ((hf_ref[...] + hr_ref[...]) * jax.nn.gelu(g_ref[...])).astype(o_ref.dtype)


def _lru_post(h_f, h_r, p):
    t, w = h_f.shape
    tm = _tile(t, (256, 128, 64, 8))
    row = pl.BlockSpec((tm, w), lambda i: (i, 0))
    return pl.pallas_call(
        _lru_post_body,
        grid=(t // tm,),
        in_specs=[row, row, row],
        out_specs=row,
        out_shape=jax.ShapeDtypeStruct((t, w), BF16),
        compiler_params=_params(1), name="lru_post",
    )(h_f, h_r, p)


def _ssd_body(x_ref, b_ref, c_ref, dtc_ref, dac_ref, dtr_ref, dar_ref, o_ref, s_ref, xw_ref, dec_ref, *, gs, r, p, rev):
    c = CHUNK
    gw = r * p

    @pl.when(pl.program_id(1) == 0)
    def _():
        s_ref[...] = jnp.zeros_like(s_ref)

    upto, _, _, row_last = _scan_masks(c, rev)
    tri = jnp.where(upto, 1.0, 0.0).astype(BF16)
    tri_t = jnp.where(_scan_masks(c, not rev)[0], 1.0, 0.0).astype(BF16)
    qs = range(gs)
    xb = x_ref[...]
    bm = [b_ref[:, q * M2_N:(q + 1) * M2_N] for q in qs]
    cm = [c_ref[:, q * M2_N:(q + 1) * M2_N] for q in qs]
    cb = [_bdot(cm[q], bm[q], _NT) for q in qs]
    cs = [_bdot(cm[q], s_ref[q]) for q in qs]
    cum_col = [_cumsum_rows(tri, dac_ref[q]) for q in qs]
    cum_row = [_cumsum_cols(dar_ref[q], tri_t) for q in qs]
    last = [cc[row_last:row_last + 1, :] for cc in cum_col]
    e_in = [jnp.exp(cc) for cc in cum_col]
    wcol = [jnp.exp(last[q] - cum_col[q]) * dtc_ref[q] for q in qs]
    e_last = [jnp.exp(l) for l in last]
    dt_row = [dtr_ref[q] for q in qs]
    qh = [(q, h) for q in qs for h in range(r)]
    sl = {(q, h): slice(q * gw + h * p, q * gw + (h + 1) * p) for q, h in qh}
    m = {(q, h): cb[q] * jnp.exp(jnp.where(upto, cum_col[q][:, h:h + 1] - cum_row[q][h:h + 1, :], -jnp.inf))
         * dt_row[q][h:h + 1, :] for q, h in qh}
    y = {k: _bdot(m[k], xb[:, sl[k]]) for k in qh}
    for q, h in qh:
        k = (q, h)
        o_ref[:, sl[k]] = y[k] + cs[q][:, h * p:(h + 1) * p] * e_in[q][:, h:h + 1]
        xw_ref[:, sl[k]] = (xb[:, sl[k]] * wcol[q][:, h:h + 1]).astype(BF16)
        dec_ref[:, sl[k]] = jnp.broadcast_to(e_last[q][:, h:h + 1], (8, p))
    for q in qs:
        gsl = slice(q * gw, (q + 1) * gw)
        s_ref[q] = s_ref[q] * dec_ref[0:1, gsl] + _bdot(bm[q], xw_ref[:, gsl], _TN)


def _ssd_scan(xbc, di, dt, da, n_ctx, rev):
    t = xbc.shape[0]
    g = M2_G
    r = di // (g * M2_P)
    gw = r * M2_P
    gs = 1
    nch, ncc = t // CHUNK, n_ctx // CHUNK
    b0, c0 = di // (gs * M2_N), (di + g * M2_N) // (gs * M2_N)
    col = lambda z: z.reshape(t, g, r).transpose(1, 0, 2)
    row = lambda z: z.reshape(nch, CHUNK, g, r).transpose(2, 0, 3, 1)
    order = lambda c: _block_order(c, ncc, nch, rev)
    xspec = pl.BlockSpec((CHUNK, gs * gw), lambda gi, c: (order(c), gi))
    bspec = pl.BlockSpec((CHUNK, gs * M2_N), lambda gi, c: (order(c), b0 + gi))
    cspec = pl.BlockSpec((CHUNK, gs * M2_N), lambda gi, c: (order(c), c0 + gi))
    tcol = pl.BlockSpec((gs, CHUNK, r), lambda gi, c: (gi, order(c), 0))
    trow = pl.BlockSpec((gs, None, r, CHUNK), lambda gi, c: (gi, order(c), 0, 0))
    return pl.pallas_call(
        functools.partial(_ssd_body, gs=gs, r=r, p=M2_P, rev=rev),
        grid=(g // gs, nch),
        in_specs=[xspec, bspec, cspec, tcol, tcol, trow, trow],
        out_specs=xspec,
        out_shape=jax.ShapeDtypeStruct((t, di), F32),
        scratch_shapes=[pltpu.VMEM((gs, M2_N, gw), F32), pltpu.VMEM((CHUNK, gs * gw), BF16),
                        pltpu.VMEM((8, gs * gw), F32)],
        compiler_params=_params(2), name="ssd_rev" if rev else "ssd_fwd",
    )(xbc, xbc, xbc, col(dt), col(da), row(dt), row(da))


def _ssd_post_body(yf_ref, yr_ref, x_ref, z_ref, dsk_ref, w_ref, o_ref):
    z = z_ref[...]
    y = (yf_ref[...] + yr_ref[...] + dsk_ref[...] * x_ref[...]) * (z * jax.nn.sigmoid(z))
    o_ref[...] = (y * lax.rsqrt(jnp.mean(y * y, axis=-1, keepdims=True) + NORM_EPS) * w_ref[...]).astype(o_ref.dtype)


def _ssd_post(y_f, y_r, xbc, zx, d_skip, norm_w):
    t, di = y_f.shape
    gw = di // M2_G
    tm = _tile(t, (256, 128, 64, 8))
    blk = pl.BlockSpec((tm, gw), lambda i, g: (i, g))
    par = pl.BlockSpec((1, gw), lambda i, g: (0, g))
    return pl.pallas_call(
        _ssd_post_body,
        grid=(t // tm, M2_G),
        in_specs=[blk, blk, blk, blk, par, par],
        out_specs=blk,
        out_shape=jax.ShapeDtypeStruct((t, di), BF16),
        compiler_params=_params(2), name="ssd_post",
    )(y_f, y_r, xbc, zx, jnp.repeat(d_skip, M2_P).reshape(1, di), norm_w.reshape(1, di))


def _to_scan(x, n_ctx, column_major):
    if not column_major:
        return x
    lat = x[n_ctx:]
    t, d = lat.shape
    lat = lat.reshape(t // GRID_W, GRID_W, d).transpose(1, 0, 2).reshape(t, d)
    return jnp.concatenate([x[:n_ctx], lat], axis=0)


def _to_grid(x, n_ctx, column_major):
    if not column_major:
        return x
    lat = x[n_ctx:]
    t, d = lat.shape
    lat = lat.reshape(GRID_W, t // GRID_W, d).transpose(1, 0, 2).reshape(t, d)
    return jnp.concatenate([x[:n_ctx], lat], axis=0)


def _rwkv7_mixer(xs, n_ctx, w_rkv, w0, w1, w2, a0, a1, a2, g1, g2, k_k, k_a, r_k, lnx_w, lnx_b):
    ts, d = xs[0].shape
    hn = d // RW_N
    r, k, v = (_linear(xs[j], w_rkv[j]) for j in range(3))
    gate = _linear(jax.nn.sigmoid(_linear(xs[5], g1)).astype(BF16), g2)
    heads = lambda z: z.reshape(ts, hn, RW_N)
    kk = heads(k * k_k)
    kk = (kk * lax.rsqrt(jnp.sum(kk * kk, axis=-1, keepdims=True) + 1e-12)).reshape(ts, d)
    kds, ys = [], []
    for dr in range(2):
        zw = _linear(jnp.tanh(_linear(xs[3], w1[dr])).astype(BF16), w2[dr])
        za = _linear(_linear(xs[4], a1[dr]).astype(BF16), a2[dr])
        kds.append(k * (1.0 + (jax.nn.sigmoid(a0[dr] + za) - 1.0) * k_a))
        ys.append(_rwkv_scan(r, k, v, kk, zw, za, w0[dr], a0[dr], k_a, n_ctx, dr == 1))
    y = heads(ys[0] + ys[1])
    mu = jnp.mean(y, axis=-1, keepdims=True)
    var = jnp.mean(jnp.square(y - mu), axis=-1, keepdims=True)
    yn = ((y - mu) * lax.rsqrt(var + RW_GN_EPS)).reshape(ts, d) * lnx_w + lnx_b
    rh = heads(r)
    bonus = (jnp.sum(rh * heads(kds[0]) * r_k, axis=-1, keepdims=True)
             + jnp.sum(rh * heads(kds[1]) * r_k, axis=-1, keepdims=True))
    return ((yn + (bonus * heads(v)).reshape(ts, d)) * gate).astype(BF16)


def _hgrn2_mixer(u, n_ctx, layer, w_in, lb_logits, norm_w):
    cum = jnp.cumsum(jax.nn.softmax(lb_logits, axis=1), axis=1)
    lb = cum[:, layer] - cum[:, 0]
    d = u.shape[1]
    p = _linear(u, w_in)
    ys = [_gla_scan(p, lb[dr], d, n_ctx, dr == 1) for dr in range(2)]
    return _gla_post(ys[0], ys[1], p, norm_w)


def _rglru_mixer(u, n_ctx, w_in, conv_w, conv_b, wa, ba, wx, bx, lam):
    log_sig = jax.nn.log_sigmoid(lam)
    p = _linear(u, w_in)
    w = p.shape[1] // 2
    xc, xcb = _conv(p, w, conv_w, conv_b, n_ctx, False, (F32, BF16))
    hs = []
    for dr in range(2):
        ra = _block_diag(xcb, wa[dr].astype(BF16))
        rx = _block_diag(xcb, wx[dr].astype(BF16))
        hs.append(_lru_scan(ra, rx, xc, ba[dr], bx[dr], log_sig[dr], n_ctx, dr == 1))
    return _lru_post(hs[0], hs[1], p)


def _mamba2_mixer(u, n_ctx, w_in, conv_w, conv_b, dt_bias, a_log, d_skip, norm_w):
    ts, d = u.shape
    di = 2 * d
    gn = M2_G * M2_N
    hh = di // M2_P
    zx = _linear(u, w_in, n_out=di + di + 2 * gn)
    dt_raw = _linear(u, w_in, col0=di + di + 2 * gn, n_out=2 * hh)
    xbc, = _conv(zx, di, conv_w, conv_b, n_ctx, True, (F32,))
    a_neg = -jnp.exp(a_log)
    dt = jax.nn.softplus(dt_raw.reshape(ts, 2, hh) + dt_bias)
    ys = [_ssd_scan(xbc, di, dt[:, dr], dt[:, dr] * a_neg[dr], n_ctx, dr == 1) for dr in range(2)]
    return _ssd_post(ys[0], ys[1], xbc, zx, d_skip, norm_w)


def _gather_body(tok_ref, x_hbm, o_ref, buf_ref, sem, *, tg):
    base = pl.program_id(0) * tg

    def row_copy(src_row, r):
        return pltpu.make_async_copy(x_hbm.at[pl.ds(src_row, 1), :], buf_ref.at[pl.ds(r, 1), :], sem)

    def start(r, carry):
        row_copy(tok_ref[base + r], r).start()
        return carry

    def wait(r, carry):
        row_copy(0, r).wait()
        return carry

    lax.fori_loop(0, tg, start, 0)
    lax.fori_loop(0, tg, wait, 0)
    o_ref[...] = buf_ref[...].astype(o_ref.dtype)


def _gather_rows(x, tok, tg):
    p, d = tok.shape[0], x.shape[1]
    return pl.pallas_call(
        functools.partial(_gather_body, tg=tg),
        grid_spec=pltpu.PrefetchScalarGridSpec(
            num_scalar_prefetch=1, grid=(p // tg,),
            in_specs=[pl.BlockSpec(memory_space=pl.ANY)],
            out_specs=pl.BlockSpec((tg, d), lambda i, tok: (i, 0)),
            scratch_shapes=[pltpu.VMEM((tg, d), x.dtype), pltpu.SemaphoreType.DMA(())]),
        out_shape=jax.ShapeDtypeStruct((p, d), BF16),
        compiler_params=_params(1), name="moe_gather",
    )(tok, x)


def _moe_up_body(te_ref, x_ref, wg_ref, wu_ref, o_ref):
    x = x_ref[...]
    g = jnp.dot(x, wg_ref[...], preferred_element_type=F32)
    u = jnp.dot(x, wu_ref[...], preferred_element_type=F32)
    o_ref[...] = (g * jax.nn.sigmoid(g) * u).astype(o_ref.dtype)


def _moe_up_sparse(xg, w_gu, tile_e, tg):
    p, d = xg.shape
    f = w_gu.shape[2] // 2
    tn = _tile(f, (512, 256, 128))
    nj = f // tn
    return pl.pallas_call(
        _moe_up_body,
        grid_spec=pltpu.PrefetchScalarGridSpec(
            num_scalar_prefetch=1, grid=(nj, p // tg),
            in_specs=[pl.BlockSpec((tg, d), lambda j, i, te: (i, 0)),
                      pl.BlockSpec((None, d, tn), lambda j, i, te: (te[i], 0, j)),
                      pl.BlockSpec((None, d, tn), lambda j, i, te: (te[i], 0, j + nj))],
            out_specs=pl.BlockSpec((tg, tn), lambda j, i, te: (i, j))),
        out_shape=jax.ShapeDtypeStruct((p, f), BF16),
        compiler_params=_params(2), name="moe_up",
    )(tile_e, xg, w_gu, w_gu)


def _moe_down_body(te_ref, a_ref, w_ref, ws_ref, o_ref):
    o_ref[...] = jnp.dot(a_ref[...], w_ref[...], preferred_element_type=F32) * ws_ref[...]


def _moe_down_sparse(act, w_down, wslot, tile_e, tg):
    p, f = act.shape
    d = w_down.shape[2]
    tn = _tile(d, (1024, 512, 256, 128))
    return pl.pallas_call(
        _moe_down_body,
        grid_spec=pltpu.PrefetchScalarGridSpec(
            num_scalar_prefetch=1, grid=(d // tn, p // tg),
            in_specs=[pl.BlockSpec((tg, f), lambda j, i, te: (i, 0)),
                      pl.BlockSpec((None, f, tn), lambda j, i, te: (te[i], 0, j)),
                      pl.BlockSpec((tg, 1), lambda j, i, te: (i, 0))],
            out_specs=pl.BlockSpec((tg, tn), lambda j, i, te: (i, j))),
        out_shape=jax.ShapeDtypeStruct((p, d), F32),
        compiler_params=_params(2), name="moe_down",
    )(tile_e, act, w_down, wslot)


def _combine_body(slot_ref, y_hbm, res_ref, g_ref, o_ref, buf_ref, sem, *, tm, n_ctx):
    base = pl.program_id(0) * tm

    def row_copy(src_row, k, r):
        return pltpu.make_async_copy(y_hbm.at[pl.ds(src_row, 1), :], buf_ref.at[k, pl.ds(r, 1), :], sem)

    def start(r, carry):
        for k in range(TOP_K):
            row_copy(slot_ref[TOP_K * (base + r) + k], k, r).start()
        return carry

    def wait(r, carry):
        for k in range(TOP_K):
            row_copy(0, k, r).wait()
        return carry

    lax.fori_loop(0, tm, start, 0)
    lax.fori_loop(0, tm, wait, 0)
    row = base + lax.broadcasted_iota(jnp.int32, (tm, 1), 0)
    gate = jnp.where(row < n_ctx, g_ref[0:1, :], g_ref[1:2, :])
    o_ref[...] = res_ref[...] + gate * (buf_ref[0] + buf_ref[1])


def _moe_combine(y, slot, res, gate, n_ctx):
    m, d = res.shape
    tm = _tile(m, (256, 128, 64, 8))
    return pl.pallas_call(
        functools.partial(_combine_body, tm=tm, n_ctx=n_ctx),
        grid_spec=pltpu.PrefetchScalarGridSpec(
            num_scalar_prefetch=1, grid=(m // tm,),
            in_specs=[pl.BlockSpec(memory_space=pl.ANY),
                      pl.BlockSpec((tm, d), lambda i, sl: (i, 0)),
                      pl.BlockSpec((2, d), lambda i, sl: (0, 0))],
            out_specs=pl.BlockSpec((tm, d), lambda i, sl: (i, 0)),
            scratch_shapes=[pltpu.VMEM((TOP_K, tm, d), F32), pltpu.SemaphoreType.DMA(())]),
        out_shape=jax.ShapeDtypeStruct((m, d), F32),
        compiler_params=_params(1), name="moe_combine",
    )(slot, y, res, gate)


def _moe(v32, router, w_gu, w_down, res, gate, n_ctx):
    m = v32.shape[0]
    e = router.shape[1]
    n_ent = TOP_K * m
    tg = _tile(n_ent, (256, 128, 64, 8))
    logits = _linear(v32, router, precision=lax.Precision.HIGHEST)
    top_val, top_idx = lax.top_k(logits, TOP_K)
    gates = jax.nn.softmax(top_val, axis=-1)
    ex = top_idx.reshape(-1)
    cnt = jnp.sum(jax.nn.one_hot(ex, e, dtype=jnp.int32), axis=0)
    padded = (cnt + tg - 1) // tg * tg
    ends = jnp.cumsum(padded)
    order = jnp.argsort(ex, stable=True)
    ex_sorted = ex[order]
    rank = jnp.arange(n_ent, dtype=jnp.int32) - (jnp.cumsum(cnt) - cnt)[ex_sorted]
    slot = jnp.zeros((n_ent,), jnp.int32).at[order].set((ends - padded)[ex_sorted] + rank)
    n_slot = n_ent + e * tg
    tok = jnp.zeros((n_slot,), jnp.int32).at[slot].set(jnp.arange(n_ent, dtype=jnp.int32) // TOP_K)
    wslot = jnp.zeros((n_slot,), F32).at[slot].set(gates.reshape(-1)).reshape(n_slot, 1)
    tile_start = jnp.arange(n_slot // tg, dtype=jnp.int32) * tg
    tile_e = jnp.minimum(jnp.sum(tile_start[:, None] >= ends[None, :], axis=1), e - 1).astype(jnp.int32)

    xg = _gather_rows(v32, tok, tg)
    act = _moe_up_sparse(xg, w_gu.astype(BF16), tile_e, tg)
    y = _moe_down_sparse(act, w_down.astype(BF16), wslot, tile_e, tg)
    return _moe_combine(y, slot, res, gate, n_ctx)


def kernel(x, c, ctx, c_ctx, mod_down, mod_up, mod_b, norm_w, final_norm_w, rw_mix, rw_w_rkv, rw_w0, rw_w1, rw_w2, rw_a0, rw_a1, rw_a2, rw_g1, rw_g2, rw_k_k, rw_k_a, rw_r_k, rw_lnx_w, rw_lnx_b, rw_w_o, hg_w_in, hg_lb, hg_norm_w, hg_w_o, rg_w_in, rg_conv_w, rg_conv_b, rg_wa, rg_ba, rg_wx, rg_bx, rg_lam, rg_w_out, m2_w_in, m2_conv_w, m2_conv_b, m2_dt_bias, m2_a_log, m2_d, m2_norm_w, m2_w_out, ffn_w_gu, ffn_w_down, moe_router, moe_w_gu, moe_w_down):
    assert x.shape[0] == 1 and c.shape[0] == 1
    depth = mod_down.shape[0]
    n_ctx, d = ctx.shape[1], ctx.shape[2]
    xh = jnp.concatenate([ctx[0], x[0]], axis=0)
    cvec = jnp.zeros((8, d), F32).at[0].set(c_ctx).at[1].set(c[0])
    cvec = jax.nn.silu(cvec).astype(BF16)
    for i in range(depth):
        kind = i % 4
        col_major = i % 2 == 1
        low = _linear(cvec, mod_down[i]).astype(BF16)
        mod = (_linear(low, mod_up[i])[:2] + mod_b[i]).reshape(2, 6, d)

        if kind == 0:
            assert not col_major
            xs = _shift_mix(xh, norm_w[i, 0], mod, rw_mix, n_ctx)
            o = _rwkv7_mixer(xs, n_ctx, rw_w_rkv, rw_w0, rw_w1, rw_w2, rw_a0, rw_a1, rw_a2,
                             rw_g1, rw_g2, rw_k_k, rw_k_a, rw_r_k, rw_lnx_w, rw_lnx_b)
            w_out = rw_w_o
        else:
            u = _to_scan(_norm_mod(xh, norm_w[i, 0], mod, 0, 1, n_ctx, (BF16,))[0], n_ctx, col_major)
            if kind == 1:
                o = _hgrn2_mixer(u, n_ctx, i, hg_w_in, hg_lb, hg_norm_w)
                w_out = hg_w_o
            elif kind == 2:
                o = _rglru_mixer(u, n_ctx, rg_w_in, rg_conv_w, rg_conv_b, rg_wa, rg_ba, rg_wx, rg_bx, rg_lam)
                w_out = rg_w_out
            else:
                o = _mamba2_mixer(u, n_ctx, m2_w_in, m2_conv_w, m2_conv_b, m2_dt_bias, m2_a_log, m2_d, m2_norm_w)
                w_out = m2_w_out
        xh = _linear_residual(_to_grid(o, n_ctx, col_major), _wt(w_out), xh, mod[:, 2], n_ctx)

        fi = i // 2
        if i % 2 == 0:
            vb, = _norm_mod(xh, norm_w[i, 1], mod, 4, 3, n_ctx, (BF16,))
            act = _swiglu_up(vb, ffn_w_gu[fi].astype(BF16))
            xh = _linear_residual(act, ffn_w_down[fi].astype(BF16), xh, mod[:, 5], n_ctx)
        else:
            v32, = _norm_mod(xh, norm_w[i, 1], mod, 4, 3, n_ctx, (F32,))
            xh = _moe(v32, moe_router[fi], moe_w_gu[fi], moe_w_down[fi], xh, mod[:, 5], n_ctx)
    out = _rmsnorm(xh[n_ctx:], final_norm_w, F32)
    return out[None]
```
